```python
import math
import jax, jax.numpy as jnp
from jax import lax
import numpy as np

D_MODEL = 1024
BATCH = 8
SEQ = 8192
DEPTH = 2

ATTN_WIDTH = D_MODEL // 2
HGRN_WIDTH = D_MODEL - ATTN_WIDTH
HEAD_DIM = 64
N_ATTN_HEADS = ATTN_WIDTH // HEAD_DIM
HGRN_EXPAND = 128
N_HGRN_HEADS = HGRN_WIDTH // HGRN_EXPAND
HGRN_HEAD_DIM = HGRN_WIDTH // N_HGRN_HEADS
DILATED_CONFIGS = ((128, 1), (512, 4), (2048, 16))
ATTN_BLOCK = 128
ROPE_THETA = 500000.0
ROPE_DIM = HEAD_DIM // 4
HGRN_CHUNK = 64
D_FF = 2816
FFN_RES_WEIGHT = 0.5
DEEPNORM_ALPHA = (2 * DEPTH) ** 0.25
DEEPNORM_BETA = (8 * DEPTH) ** -0.25
LN_EPS = 1e-5
RMS_EPS = 1e-6
N_SUBLAYERS = 3
IN_SPLITS = [int(s) for s in np.cumsum([ATTN_WIDTH] * 3 + [HGRN_WIDTH] * 3)]
IN_COLS = 3 * ATTN_WIDTH + 4 * HGRN_WIDTH

kernel_name = 'hybrid_dilated_attn_hgrn2_macaron'


def layer_norm(x, g, b):
    xf = x.astype(jnp.float32)
    mu = jnp.mean(xf, axis=-1, keepdims=True)
    var = jnp.mean(jnp.square(xf - mu), axis=-1, keepdims=True)
    return ((xf - mu) * lax.rsqrt(var + LN_EPS) * g + b).astype(x.dtype)


def modulate(x, shift, scale):
    return x * (1 + scale[:, None, :]) + shift[:, None, :]


def swiglu_ffn(h, w_in, w_out):
    g, u = jnp.split(h @ w_in, 2, axis=-1)
    return (jax.nn.silu(g) * u) @ w_out


def partial_rope(t, pos):
    half = ROPE_DIM // 2
    inv_freq = ROPE_THETA ** (-jnp.arange(half, dtype=jnp.float32) * 2.0 / ROPE_DIM)
    ang = pos.astype(jnp.float32)[:, None, :, None] * inv_freq
    cos = jnp.cos(ang).astype(t.dtype)
    sin = jnp.sin(ang).astype(t.dtype)
    t1, t2, rest = t[..., :half], t[..., half:ROPE_DIM], t[..., ROPE_DIM:]
    return jnp.concatenate([t1 * cos - t2 * sin, t2 * cos + t1 * sin, rest], axis=-1)


def dilated_branch(q, k, v, window, dilation):
    B, H, S, hd = q.shape
    L = S // dilation
    wm = window // dilation
    nb = -(-L // ATTN_BLOCK)
    lp = nb * ATTN_BLOCK

    def res_view(t):
        return t.reshape(B, H, L, dilation, hd).transpose(0, 1, 3, 2, 4)

    qr = jnp.pad(res_view(q), ((0, 0), (0, 0), (0, 0), (0, lp - L), (0, 0)))
    kr = jnp.pad(res_view(k), ((0, 0), (0, 0), (0, 0), (ATTN_BLOCK, lp - L), (0, 0)))
    vr = jnp.pad(res_view(v), ((0, 0), (0, 0), (0, 0), (ATTN_BLOCK, lp - L), (0, 0)))

    def banded(t):
        prev = t[..., :lp, :].reshape(B, H, dilation, nb, ATTN_BLOCK, hd)
        cur = t[..., ATTN_BLOCK:, :].reshape(B, H, dilation, nb, ATTN_BLOCK, hd)
        return jnp.concatenate([prev, cur], axis=-2)

    kb, vb = banded(kr), banded(vr)
    qb = qr.reshape(B, H, dilation, nb, ATTN_BLOCK, hd)
    s = jnp.einsum('bhrnqe,bhrnke->bhrnqk', qb, kb, preferred_element_type=jnp.float32)
    qi = jnp.arange(ATTN_BLOCK)[:, None] + ATTN_BLOCK
    ki = jnp.arange(2 * ATTN_BLOCK)[None, :]
    dist = qi - ki
    key_m = jnp.arange(nb)[:, None, None] * ATTN_BLOCK - ATTN_BLOCK + ki[None]
    mask = (dist >= 0) & (dist <= wm) & (key_m >= 0)
    s = jnp.where(mask, s, -jnp.inf)
    mx = jnp.max(s, axis=-1, keepdims=True)
    p = jnp.exp(s - mx)
    den = jnp.sum(p, axis=-1, keepdims=True)
    o = jnp.einsum('bhrnqk,bhrnke->bhrnqe', p.astype(v.dtype), vb,
                   preferred_element_type=jnp.float32) / den
    lse = (mx + jnp.log(den))[..., 0]
    o = o.reshape(B, H, dilation, lp, hd)[:, :, :, :L].transpose(0, 1, 3, 2, 4).reshape(B, H, S, hd)
    lse = lse.reshape(B, H, dilation, lp)[:, :, :, :L].transpose(0, 1, 3, 2).reshape(B, H, S)
    return o, lse


def dilated_attention(q, k, v):
    outs, lses = zip(*[dilated_branch(q, k, v, w, d) for (w, d) in DILATED_CONFIGS])
    wts = jax.nn.softmax(jnp.stack(lses), axis=0)
    return jnp.sum(jnp.stack(outs) * wts[..., None], axis=0)


def hgrn2_scan(q, k, v, log_f):
    B, S, H, dk = q.shape
    dv = v.shape[-1]
    nc = S // HGRN_CHUNK

    def chunks(t):
        return t.reshape(B, nc, HGRN_CHUNK, H, t.shape[-1]).transpose(1, 0, 3, 2, 4)

    causal = jnp.tril(jnp.ones((HGRN_CHUNK, HGRN_CHUNK), dtype=bool))[:, :, None]

    def step(state, inp):
        qc, kc, vc, ac = inp
        b = jnp.cumsum(ac, axis=2)
        decay = jnp.exp(jnp.where(causal, b[:, :, :, None, :] - b[:, :, None, :, :], -jnp.inf))
        attn = jnp.einsum('bhtd,bhsd,bhtsd->bhts', qc, kc, decay)
        o = (jnp.einsum('bhts,bhse->bhte', attn, vc)
             + jnp.einsum('bhtd,bhde->bhte', qc * jnp.exp(b), state))
        b_last = b[:, :, -1:, :]
        state = (jnp.exp(b_last[:, :, 0, :, None]) * state
                 + jnp.einsum('bhsd,bhse->bhde', kc * jnp.exp(b_last - b), vc))
        return state, o

    state0 = jnp.zeros((B, H, dk, dv), jnp.float32)
    _, o = lax.scan(step, state0, (chunks(q), chunks(k), chunks(v), chunks(log_f)))
    return o.transpose(1, 0, 3, 2, 4).reshape(B, S, H, dv)


def hgrn2_mixer(hq, hf, hi, hg, lb, norm_w):
    B, S, _ = hq.shape
    shp = (B, S, N_HGRN_HEADS, HGRN_EXPAND)
    q = jax.nn.silu(hq.astype(jnp.float32)).reshape(shp)
    fp = hf.astype(jnp.float32).reshape(shp)
    log_f = jnp.logaddexp(jnp.log(lb), jnp.log1p(-lb) + jax.nn.log_sigmoid(fp))
    k = (1 - lb) * jax.nn.sigmoid(-fp)
    v = hi.astype(jnp.float32).reshape(B, S, N_HGRN_HEADS, HGRN_HEAD_DIM)
    o = hgrn2_scan(q, k, v, log_f)
    o = o * lax.rsqrt(jnp.mean(jnp.square(o), axis=-1, keepdims=True) + RMS_EPS)
    o = o * norm_w.reshape(N_HGRN_HEADS, HGRN_HEAD_DIM)
    o = o.reshape(B, S, HGRN_WIDTH) * jax.nn.silu(hg.astype(jnp.float32))
    return o.astype(hq.dtype)


def hybrid_mixer(h, pos, w_in, w_out, norm_w, lb):
    B, S, _ = h.shape
    aq, ak, av, hq, hf, hi, hg = jnp.split(h @ w_in, IN_SPLITS, axis=-1)

    def heads(t):
        return t.reshape(B, S, N_ATTN_HEADS, HEAD_DIM).transpose(0, 2, 1, 3)

    q = partial_rope(heads(aq), pos) * (HEAD_DIM ** -0.5)
    k = partial_rope(heads(ak), pos)
    v = heads(av)
    ao = dilated_attention(q, k, v).astype(h.dtype).transpose(0, 2, 1, 3).reshape(B, S, ATTN_WIDTH)
    go = hgrn2_mixer(hq, hf, hi, hg, lb, norm_w)
    return jnp.concatenate([ao, go], axis=-1) @ w_out


def post_norm_update(x, y, gate, g, b, res_weight):
    return layer_norm(DEEPNORM_ALPHA * x + res_weight * (1 + gate[:, None, :]) * y, g, b)


def setup_inputs(seed: int = 0) -> dict:
    key = jax.random.key(seed)
    ks = jax.random.split(key, 16)
    nrm = jax.random.normal
    x = nrm(ks[0], (BATCH, SEQ, D_MODEL), jnp.float32)
    c = nrm(ks[1], (BATCH, D_MODEL), jnp.float32)
    positions = jnp.broadcast_to(jnp.arange(SEQ, dtype=jnp.int32)[None, :], (BATCH, SEQ))
    ln_g = 1.0 + 0.02 * nrm(ks[2], (DEPTH, N_SUBLAYERS, D_MODEL), jnp.float32)
    ln_b = 0.02 * nrm(ks[3], (DEPTH, N_SUBLAYERS, D_MODEL), jnp.float32)
    ada_w = nrm(ks[4], (DEPTH, D_MODEL, N_SUBLAYERS * 3 * D_MODEL), jnp.float32) * (0.1 * D_MODEL ** -0.5)
    ada_b = 0.02 * nrm(ks[5], (DEPTH, N_SUBLAYERS * 3 * D_MODEL), jnp.float32)
    ffn1_w_in = nrm(ks[6], (DEPTH, D_MODEL, 2 * D_FF), jnp.float32) * D_MODEL ** -0.5
    ffn1_w_out = nrm(ks[7], (DEPTH, D_FF, D_MODEL), jnp.float32) * (D_FF ** -0.5 * DEEPNORM_BETA)
    ffn2_w_in = nrm(ks[8], (DEPTH, D_MODEL, 2 * D_FF), jnp.float32) * D_MODEL ** -0.5
    ffn2_w_out = nrm(ks[9], (DEPTH, D_FF, D_MODEL), jnp.float32) * (D_FF ** -0.5 * DEEPNORM_BETA)
    mix_w_in = nrm(ks[10], (DEPTH, D_MODEL, IN_COLS), jnp.float32) * D_MODEL ** -0.5
    mix_w_out = nrm(ks[11], (DEPTH, D_MODEL, D_MODEL), jnp.float32) * (D_MODEL ** -0.5 * DEEPNORM_BETA)
    hgrn_norm_w = 1.0 + 0.02 * nrm(ks[12], (DEPTH, HGRN_WIDTH), jnp.float32)
    hgrn_lb_logits = 0.5 * nrm(ks[13], (DEPTH, HGRN_WIDTH), jnp.float32)
    return {'x': x, 'c': c, 'positions': positions, 'ln_g': ln_g, 'ln_b': ln_b,
            'ada_w': ada_w, 'ada_b': ada_b,
            'ffn1_w_in': ffn1_w_in, 'ffn1_w_out': ffn1_w_out,
            'ffn2_w_in': ffn2_w_in, 'ffn2_w_out': ffn2_w_out,
            'mix_w_in': mix_w_in, 'mix_w_out': mix_w_out,
            'hgrn_norm_w': hgrn_norm_w, 'hgrn_lb_logits': hgrn_lb_logits}


def reference(x, c, positions, ln_g, ln_b, ada_w, ada_b, ffn1_w_in, ffn1_w_out,
              ffn2_w_in, ffn2_w_out, mix_w_in, mix_w_out, hgrn_norm_w, hgrn_lb_logits):
    B = x.shape[0]
    lb_all = jnp.cumsum(jax.nn.softmax(hgrn_lb_logits.astype(jnp.float32), axis=0), axis=0)
    lb_all = lb_all - lb_all[0:1]
    cond = jax.nn.silu(c)
    for l in range(DEPTH):
        ada = (cond @ ada_w[l] + ada_b[l]).reshape(B, N_SUBLAYERS, 3, D_MODEL)
        h = modulate(x, ada[:, 0, 0], ada[:, 0, 1])
        x = post_norm_update(x, swiglu_ffn(h, ffn1_w_in[l], ffn1_w_out[l]), ada[:, 0, 2],
                             ln_g[l, 0], ln_b[l, 0], FFN_RES_WEIGHT)
        h = modulate(x, ada[:, 1, 0], ada[:, 1, 1])
        lb = lb_all[l].reshape(N_HGRN_HEADS, HGRN_EXPAND)
        y = hybrid_mixer(h, positions, mix_w_in[l], mix_w_out[l], hgrn_norm_w[l], lb)
        x = post_norm_update(x, y, ada[:, 1, 2], ln_g[l, 1], ln_b[l, 1], 1.0)
        h = modulate(x, ada[:, 2, 0], ada[:, 2, 1])
        x = post_norm_update(x, swiglu_ffn(h, ffn2_w_in[l], ffn2_w_out[l]), ada[:, 2, 2],
                             ln_g[l, 2], ln_b[l, 2], FFN_RES_WEIGHT)
    return x
```

```python
import functools
import math

import numpy as np
import jax
import jax.numpy as jnp
from jax import lax
from jax.experimental import pallas as pl
from jax.experimental.pallas import tpu as pltpu

D_MODEL = 1024
DEPTH = 2
N_SUBLAYERS = 3
ATTN_WIDTH = 512
HGRN_WIDTH = 512
HEAD_DIM = 64
LANES = 128
N_SLABS = ATTN_WIDTH // LANES
ROPE_THETA = 500000.0
ROPE_DIM = HEAD_DIM // 4
ROPE_HALF = ROPE_DIM // 2
D_FF = 2816
FFN_RES_WEIGHT = 0.5
DEEPNORM_ALPHA = (2 * DEPTH) ** 0.25
LN_EPS = 1e-5
RMS_EPS = 1e-6
IN_COLS = 3 * ATTN_WIDTH + 4 * HGRN_WIDTH
DILATIONS = (1, 4, 16)
BAND = 128
ATTN_BLOCK = 128
ATTN_TILE = ATTN_BLOCK * DILATIONS[-1]
HGRN_CHUNK = 128
HGRN_LEVELS = 7
HGRN_TILE = 512
ROW_TILE = 512
FF_CHUNK = 256
ADA_COLS = N_SUBLAYERS * 3 * D_MODEL
NEG_BIG = -1e30
VMEM_LIMIT = 56 * 1024 * 1024

F32 = jnp.float32
BF16 = jnp.bfloat16


def _sigmoid(x):
    return 1.0 / (1.0 + jnp.exp(-x))


def _layer_norm(z, g, b):
    mu = jnp.mean(z, axis=-1, keepdims=True)
    zc = z - mu
    var = jnp.mean(zc * zc, axis=-1, keepdims=True)
    return zc * lax.rsqrt(var + LN_EPS) * g + b


def _resident(shape):
    zeros = (0,) * len(shape)
    return pl.BlockSpec(shape, lambda *_: zeros, pipeline_mode=pl.Buffered(1))


def _ada_kernel(c_ref, w_ref, b_ref, o_ref):
    c = c_ref[...]
    cond = c * _sigmoid(c)
    o_ref[0] = jnp.dot(cond, w_ref[0], preferred_element_type=F32,
                       precision=lax.Precision.HIGHEST) + b_ref[0]


def _ada_call(c, ada_w, ada_b):
    B = c.shape[0]
    ncol = ADA_COLS // D_MODEL
    return pl.pallas_call(
        _ada_kernel,
        grid=(DEPTH, ncol),
        in_specs=[
            pl.BlockSpec((B, D_MODEL), lambda l, j: (0, 0)),
            pl.BlockSpec((1, D_MODEL, D_MODEL), lambda l, j: (l, 0, j)),
            pl.BlockSpec((1, 1, D_MODEL), lambda l, j: (l, 0, j)),
        ],
        out_specs=pl.BlockSpec((1, B, D_MODEL), lambda l, j: (l, 0, j)),
        out_shape=jax.ShapeDtypeStruct((DEPTH, B, ADA_COLS), F32),
        compiler_params=pltpu.CompilerParams(
            dimension_semantics=("parallel", "parallel"), vmem_limit_bytes=VMEM_LIMIT),
        name="ada",
    )(c, ada_w, ada_b.reshape(DEPTH, 1, ADA_COLS))


def _rope_kernel(pos_ref, cos_ref, sin_up_ref, sin_dn_ref):
    pos = pos_ref[0].astype(F32)
    lane = lax.broadcasted_iota(jnp.int32, (1, LANES), 1)
    in_head = lane & (HEAD_DIM - 1)
    freq = (in_head & (ROPE_HALF - 1)).astype(F32)
    inv_freq = jnp.exp(freq * (-math.log(ROPE_THETA) * 2.0 / ROPE_DIM))
    ang = pos * inv_freq
    cos = jnp.cos(ang)
    sin = jnp.sin(ang)
    first = in_head < ROPE_HALF
    second = jnp.logical_and(in_head >= ROPE_HALF, in_head < ROPE_DIM)
    cos_ref[0] = jnp.where(in_head < ROPE_DIM, cos, 1.0)
    sin_up_ref[0] = jnp.where(first, -sin, 0.0)
    sin_dn_ref[0] = jnp.where(second, sin, 0.0)


def _rope_call(positions):
    B, S = positions.shape
    rows = min(S, 1024)
    spec = pl.BlockSpec((1, rows, LANES), lambda b, i: (b, i, 0))
    shape = jax.ShapeDtypeStruct((B, S, LANES), F32)
    return pl.pallas_call(
        _rope_kernel,
        grid=(B, S // rows),
        in_specs=[pl.BlockSpec((1, rows, 1), lambda b, i: (b, i, 0))],
        out_specs=[spec, spec, spec],
        out_shape=[shape, shape, shape],
        compiler_params=pltpu.CompilerParams(
            dimension_semantics=("parallel", "parallel"), vmem_limit_bytes=VMEM_LIMIT),
        name="rope_tables",
    )(positions.reshape(B, S, 1))


def _ffn_kernel(x_ref, shift_ref, scale_ref, gate_ref, win_ref, wout_ref, g_ref, b_ref,
                o_ref, act_ref):
    x = x_ref[0]
    h = (x * (1.0 + scale_ref[0]) + shift_ref[0]).astype(BF16)
    for j in range(D_FF // FF_CHUNK):
        lo = j * FF_CHUNK
        g = jnp.dot(h, win_ref[:, lo:lo + FF_CHUNK], preferred_element_type=F32)
        u = jnp.dot(h, win_ref[:, D_FF + lo:D_FF + lo + FF_CHUNK], preferred_element_type=F32)
        act_ref[:, lo:lo + FF_CHUNK] = (g * _sigmoid(g) * u).astype(BF16)
    y = jnp.dot(act_ref[...], wout_ref[...], preferred_element_type=F32)
    z = DEEPNORM_ALPHA * x + (FFN_RES_WEIGHT * (1.0 + gate_ref[0])) * y
    o_ref[0] = _layer_norm(z, g_ref[...], b_ref[...])


def _ffn_call(x, shift, scale, gate, w_in, w_out, ln_g, ln_b):
    B, S, D = x.shape
    rows = min(S, ROW_TILE)
    tile = pl.BlockSpec((1, rows, D), lambda b, i: (b, i, 0))
    per_batch = pl.BlockSpec((1, 1, D), lambda b, i: (b, 0, 0))
    return pl.pallas_call(
        _ffn_kernel,
        grid=(B, S // rows),
        in_specs=[tile, per_batch, per_batch, per_batch,
                  _resident((D, 2 * D_FF)), _resident((D_FF, D)),
                  _resident((1, D)), _resident((1, D))],
        out_specs=tile,
        out_shape=jax.ShapeDtypeStruct((B, S, D), F32),
        scratch_shapes=[pltpu.VMEM((rows, D_FF), BF16)],
        compiler_params=pltpu.CompilerParams(
            dimension_semantics=("parallel", "parallel"), vmem_limit_bytes=VMEM_LIMIT),
        name="ffn",
    )(x, shift, scale, gate, w_in, w_out, ln_g.reshape(1, D), ln_b.reshape(1, D))


def _proj_kernel(x_ref, shift_ref, scale_ref, w_ref, cos_ref, sup_ref, sdn_ref,
                 q_ref, k_ref, v_ref, hq_ref, hf_ref, hi_ref, hg_ref):
    x = x_ref[0]
    h = (x * (1.0 + scale_ref[0]) + shift_ref[0]).astype(BF16)
    cos, sup, sdn = cos_ref[0], sup_ref[0], sdn_ref[0]
    outs = (q_ref, k_ref, v_ref, hq_ref, hf_ref, hi_ref, hg_ref)
    for part, out in enumerate(outs):
        p = jnp.dot(h, w_ref[:, part * ATTN_WIDTH:(part + 1) * ATTN_WIDTH],
                    preferred_element_type=F32)
        for s in range(N_SLABS):
            slab = p[:, s * LANES:(s + 1) * LANES]
            if part < 2:
                slab = (slab * cos + pltpu.roll(slab, LANES - ROPE_HALF, 1) * sup
                        + pltpu.roll(slab, ROPE_HALF, 1) * sdn)
                if part == 0:
                    slab = slab * (HEAD_DIM ** -0.5)
            out[0, s] = slab.astype(out.dtype)


def _proj_call(x, shift, scale, w, cos, sup, sdn):
    B, S, D = x.shape
    rows = min(S, ROW_TILE)
    tile = pl.BlockSpec((1, rows, D), lambda b, i: (b, i, 0))
    per_batch = pl.BlockSpec((1, 1, D), lambda b, i: (b, 0, 0))
    table = pl.BlockSpec((1, rows, LANES), lambda b, i: (b, i, 0))
    out_spec = pl.BlockSpec((1, N_SLABS, rows, LANES), lambda b, i: (b, 0, i, 0))
    dtypes = (BF16, BF16, BF16, BF16, F32, BF16, BF16)
    return pl.pallas_call(
        _proj_kernel,
        grid=(B, S // rows),
        in_specs=[tile, per_batch, per_batch, _resident((D, IN_COLS)), table, table, table],
        out_specs=[out_spec] * 7,
        out_shape=[jax.ShapeDtypeStruct((B, N_SLABS, S, LANES), dt) for dt in dtypes],
        compiler_params=pltpu.CompilerParams(
            dimension_semantics=("parallel", "parallel"), vmem_limit_bytes=VMEM_LIMIT),
        name="mix_proj",
    )(x, shift, scale, w, cos, sup, sdn)


def _attn_kernel(q1, k1c, k1p, v1c, v1p, q4, k4c, k4p, v4c, v4p, q16, k16c, k16p, v16c, v16p,
                 o_ref, bias_ref, m_ref, l_ref, acc_ref):
    first_tile = pl.program_id(2) == 0
    lane = lax.broadcasted_iota(jnp.int32, (1, LANES), 1)
    head0 = lane < HEAD_DIM

    row = lax.broadcasted_iota(jnp.int32, (2 * ATTN_BLOCK, 2 * ATTN_BLOCK), 0) & (ATTN_BLOCK - 1)
    col = lax.broadcasted_iota(jnp.int32, (2 * ATTN_BLOCK, 2 * ATTN_BLOCK), 1)
    visible = jnp.logical_and(col >= row, col <= row + BAND)
    bias_ref[0] = jnp.where(visible, 0.0, NEG_BIG)
    no_prev = jnp.logical_and(visible, col >= jnp.where(first_tile, ATTN_BLOCK, 0))
    bias_ref[1] = jnp.where(no_prev, 0.0, NEG_BIG)

    def block_stats(q, k2, v2, which_bias):
        zero = jnp.zeros_like(q)
        qq = jnp.concatenate([jnp.where(head0, q, zero), jnp.where(head0, zero, q)], axis=0)
        s = lax.dot_general(qq, k2, (((1,), (1,)), ((), ())), preferred_element_type=F32)
        s = s + bias_ref[which_bias]
        m = jnp.max(s, axis=-1, keepdims=True)
        p = jnp.exp(s - m)
        l = jnp.sum(p, axis=-1, keepdims=True)
        pv = jnp.dot(p.astype(BF16), v2, preferred_element_type=F32)
        acc_b = jnp.where(head0, pv[:ATTN_BLOCK], pv[ATTN_BLOCK:])
        m_b = jnp.where(head0, m[:ATTN_BLOCK], m[ATTN_BLOCK:])
        l_b = jnp.where(head0, l[:ATTN_BLOCK], l[ATTN_BLOCK:])
        return m_b, l_b, acc_b

    branches = ((1, q1, k1c, k1p, v1c, v1p), (4, q4, k4c, k4p, v4c, v4p),
                (16, q16, k16c, k16p, v16c, v16p))
    for d, q_ref, kc_ref, kp_ref, vc_ref, vp_ref in branches:
        for r in range(d):
            lanes = slice(r * LANES, (r + 1) * LANES)
            for nb in range(ATTN_TILE // d // ATTN_BLOCK):
                lo = nb * ATTN_BLOCK
                q = q_ref[0, 0, lo:lo + ATTN_BLOCK, lanes]
                if nb == 0:
                    k2 = jnp.concatenate([kp_ref[0, 0, :, lanes], kc_ref[0, 0, :ATTN_BLOCK, lanes]], axis=0)
                    v2 = jnp.concatenate([vp_ref[0, 0, :, lanes], vc_ref[0, 0, :ATTN_BLOCK, lanes]], axis=0)
                else:
                    k2 = kc_ref[0, 0, lo - ATTN_BLOCK:lo + ATTN_BLOCK, lanes]
                    v2 = vc_ref[0, 0, lo - ATTN_BLOCK:lo + ATTN_BLOCK, lanes]
                m_b, l_b, acc_b = block_stats(q, k2, v2, 1 if nb == 0 else 0)
                if d == 1:
                    rows = pl.ds(lo, ATTN_BLOCK)
                    m_ref[rows, :] = m_b
                    l_ref[rows, :] = l_b
                    acc_ref[rows, :] = acc_b
                else:
                    rows = pl.ds(r + d * lo, ATTN_BLOCK, stride=d)
                    m_old = m_ref[rows, :]
                    m_new = jnp.maximum(m_old, m_b)
                    w_old = jnp.exp(m_old - m_new)
                    w_blk = jnp.exp(m_b - m_new)
                    m_ref[rows, :] = m_new
                    l_ref[rows, :] = w_old * l_ref[rows, :] + w_blk * l_b
                    acc_ref[rows, :] = w_old * acc_ref[rows, :] + w_blk * acc_b
    o_ref[0] = (acc_ref[...] / l_ref[...]).astype(o_ref.dtype)


def _attn_call(q, k, v):
    B, _, S, _ = q.shape
    n_tiles = S // ATTN_TILE
    in_specs, args = [], []
    for d in DILATIONS:
        L, width = S // d, d * LANES
        rows = ATTN_TILE // d
        cur = pl.BlockSpec((1, 1, rows, width), lambda b, s, i: (b, s, i, 0))
        per_tile = rows // ATTN_BLOCK
        prev = pl.BlockSpec((1, 1, ATTN_BLOCK, width),
                            lambda b, s, i, n=per_tile: (b, s, jnp.maximum(i * n - 1, 0), 0))
        qd, kd, vd = (t.reshape(B, N_SLABS, L, width) for t in (q, k, v))
        in_specs += [cur, cur, prev, cur, prev]
        args += [qd, kd, kd, vd, vd]
    return pl.pallas_call(
        _attn_kernel,
        grid=(B, N_SLABS, n_tiles),
        in_specs=in_specs,
        out_specs=pl.BlockSpec((1, ATTN_TILE, LANES), lambda b, s, i: (b, i, s)),
        out_shape=jax.ShapeDtypeStruct((B, S, ATTN_WIDTH), BF16),
        scratch_shapes=[pltpu.VMEM((2, 2 * ATTN_BLOCK, 2 * ATTN_BLOCK), F32),
                        pltpu.VMEM((ATTN_TILE, LANES), F32),
                        pltpu.VMEM((ATTN_TILE, LANES), F32),
                        pltpu.VMEM((ATTN_TILE, LANES), F32)],
        compiler_params=pltpu.CompilerParams(
            dimension_semantics=("parallel", "parallel", "parallel"), vmem_limit_bytes=VMEM_LIMIT),
        name="dilated_attn",
    )(*args)


def _hgrn_scan_matrices():
    C = HGRN_CHUNK
    t = np.arange(C)[:, None]
    u = np.arange(C)[None, :]
    mats = []
    for lvl in range(HGRN_LEVELS + 1):
        start = (t >> lvl) << lvl
        mats.append((u >= start) & (u <= t))
    for lvl in range(1, HGRN_LEVELS + 1):
        end = ((t >> lvl) << lvl) + (1 << lvl) - 1
        mats.append((u > t) & (u <= end))
    return np.concatenate(mats, axis=0).astype(np.float32)


def _hgrn_kernel(hq_ref, hf_ref, hi_ref, hg_ref, loglb_ref, log1mlb_ref, onemlb_ref, nw_ref,
                 scan_ref, o_ref, state_ref):
    C = HGRN_CHUNK

    @pl.when(pl.program_id(2) == 0)
    def _():
        state_ref[...] = jnp.zeros_like(state_ref)

    log_lb, log1m_lb, one_m_lb, norm_w = loglb_ref[0], log1mlb_ref[0], onemlb_ref[0], nw_ref[0]

    t_idx = lax.broadcasted_iota(jnp.int32, (C, C), 0)
    s_idx = lax.broadcasted_iota(jnp.int32, (C, C), 1)
    differ = t_idx ^ s_idx
    later = t_idx > s_idx
    level_mask = [jnp.logical_and(later, (differ >> lvl) == 1) for lvl in range(HGRN_LEVELS)]
    diagonal = t_idx == s_idx

    def nt_dot(a, b):
        return lax.dot_general(a, b, (((1,), (1,)), ((), ())), preferred_element_type=F32)

    for c in range(hq_ref.shape[2] // C):
        rows = slice(c * C, (c + 1) * C)
        fp = hf_ref[0, 0, rows, :]
        hq = hq_ref[0, 0, rows, :].astype(F32)
        hg = hg_ref[0, 0, rows, :].astype(F32)
        v = hi_ref[0, 0, rows, :]

        log_sig = jnp.minimum(fp, 0.0) - jnp.log(1.0 + jnp.exp(-jnp.abs(fp)))
        l2 = log1m_lb + log_sig
        hi_l = jnp.maximum(log_lb, l2)
        a = hi_l + jnp.log(1.0 + jnp.exp(-jnp.abs(log_lb - l2)))
        k = one_m_lb * jnp.exp(log_sig - fp)
        q = hq * _sigmoid(hq)

        a_hi = a.astype(BF16)
        a_lo = (a - a_hi.astype(F32)).astype(BF16)
        seg = jnp.dot(scan_ref[...], jnp.concatenate([a_hi, a_lo], axis=1),
                      preferred_element_type=F32)
        seg = seg[:, :LANES] + seg[:, LANES:]

        def q_decay(lvl):
            return seg[lvl * C:(lvl + 1) * C]

        def k_decay(lvl):
            base = (HGRN_LEVELS + lvl) * C
            return seg[base:base + C]

        q_bf = q.astype(BF16)
        k_bf = k.astype(BF16)
        attn = jnp.where(diagonal, nt_dot(q_bf, k_bf), 0.0)
        for lvl in range(HGRN_LEVELS):
            q_l = (q * jnp.exp(q_decay(lvl))).astype(BF16)
            k_l = k_bf if lvl == 0 else (k * jnp.exp(k_decay(lvl))).astype(BF16)
            attn = jnp.where(level_mask[lvl], nt_dot(q_l, k_l), attn)

        state = state_ref[...]
        q_in = (q * jnp.exp(q_decay(HGRN_LEVELS))).astype(BF16)
        o = (jnp.dot(attn.astype(BF16), v, preferred_element_type=F32)
             + nt_dot(q_in, state.astype(BF16)))
        k_out = (k * jnp.exp(k_decay(HGRN_LEVELS))).astype(BF16)
        chunk_decay = jnp.exp(q_decay(HGRN_LEVELS)[C - 1:C, :])
        state_ref[...] = state * chunk_decay + lax.dot_general(
            v, k_out, (((0,), (0,)), ((), ())), preferred_element_type=F32)

        o = o * lax.rsqrt(jnp.mean(o * o, axis=-1, keepdims=True) + RMS_EPS)
        o = o * norm_w * (hg * _sigmoid(hg))
        o_ref[0, rows, :] = o.astype(o_ref.dtype)


def _hgrn_call(hq, hf, hi, hg, log_lb, log1m_lb, one_m_lb, norm_w):
    B, H, S, _ = hq.shape
    rows = min(S, HGRN_TILE)
    tile = pl.BlockSpec((1, 1, rows, LANES), lambda b, h, i: (b, h, i, 0))
    per_head = pl.BlockSpec((1, 1, LANES), lambda b, h, i: (h, 0, 0))
    scan = jnp.asarray(_hgrn_scan_matrices(), dtype=BF16)
    return pl.pallas_call(
        _hgrn_kernel,
        grid=(B, H, S // rows),
        in_specs=[tile, tile, tile, tile, per_head, per_head, per_head, per_head,
                  _resident(scan.shape)],
        out_specs=pl.BlockSpec((1, rows, LANES), lambda b, h, i: (b, i, h)),
        out_shape=jax.ShapeDtypeStruct((B, S, HGRN_WIDTH), BF16),
        scratch_shapes=[pltpu.VMEM((LANES, LANES), F32)],
        compiler_params=pltpu.CompilerParams(
            dimension_semantics=("parallel", "parallel", "arbitrary"), vmem_limit_bytes=VMEM_LIMIT),
        name="hgrn2",
    )(hq, hf, hi, hg, log_lb, log1m_lb, one_m_lb, norm_w, scan)


def _mix_out_kernel(x_ref, ao_ref, go_ref, gate_ref, w_ref, g_ref, b_ref, o_ref):
    x = x_ref[0]
    y = (jnp.dot(ao_ref[0], w_ref[:ATTN_WIDTH, :], preferred_element_type=F32)
         + jnp.dot(go_ref[0], w_ref[ATTN_WIDTH:, :], preferred_element_type=F32))
    z = DEEPNORM_ALPHA * x + (1.0 + gate_ref[0]) * y
    o_ref[0] = _layer_norm(z, g_ref[...], b_ref[...])


def _mix_out_call(x, ao, go, gate, w_out, ln_g, ln_b):
    B, S, D = x.shape
    rows = min(S, ROW_TILE)
    tile = pl.BlockSpec((1, rows, D), lambda b, i: (b, i, 0))
    half = pl.BlockSpec((1, rows, ATTN_WIDTH), lambda b, i: (b, i, 0))
    per_batch = pl.BlockSpec((1, 1, D), lambda b, i: (b, 0, 0))
    return pl.pallas_call(
        _mix_out_kernel,
        grid=(B, S // rows),
        in_specs=[tile, half, half, per_batch, _resident((D, D)), _resident((1, D)), _resident((1, D))],
        out_specs=tile,
        out_shape=jax.ShapeDtypeStruct((B, S, D), F32),
        compiler_params=pltpu.CompilerParams(
            dimension_semantics=("parallel", "parallel"), vmem_limit_bytes=VMEM_LIMIT),
        name="mix_out",
    )(x, ao, go, gate, w_out, ln_g.reshape(1, D), ln_b.reshape(1, D))


def kernel(x, c, positions, ln_g, ln_b, ada_w, ada_b, ffn1_w_in, ffn1_w_out, ffn2_w_in, ffn2_w_out,
           mix_w_in, mix_w_out, hgrn_norm_w, hgrn_lb_logits):
    B, S, D = x.shape
    assert D == D_MODEL and S % ATTN_TILE == 0 and S % HGRN_TILE == 0

    lb_all = jnp.cumsum(jax.nn.softmax(hgrn_lb_logits.astype(F32), axis=0), axis=0)
    lb_all = (lb_all - lb_all[0:1]).reshape(DEPTH, N_SLABS, 1, LANES)
    log_lb, log1m_lb, one_m_lb = jnp.log(lb_all), jnp.log1p(-lb_all), 1.0 - lb_all
    norm_w = hgrn_norm_w.astype(F32).reshape(DEPTH, N_SLABS, 1, LANES)

    ada = _ada_call(c, ada_w, ada_b).reshape(DEPTH, B, N_SUBLAYERS, 3, 1, D)
    cos, sin_up, sin_dn = _rope_call(positions)

    for l in range(DEPTH):
        def mod(sub, kind, l=l):
            return ada[l, :, sub, kind]

        x = _ffn_call(x, mod(0, 0), mod(0, 1), mod(0, 2), ffn1_w_in[l].astype(BF16),
                      ffn1_w_out[l].astype(BF16), ln_g[l, 0], ln_b[l, 0])

        q, k, v, hq, hf, hi, hg = _proj_call(x, mod(1, 0), mod(1, 1), mix_w_in[l].astype(BF16),
                                             cos, sin_up, sin_dn)
        ao = _attn_call(q, k, v)
        go = _hgrn_call(hq, hf, hi, hg, log_lb[l], log1m_lb[l], one_m_lb[l], norm_w[l])
        x = _mix_out_call(x, ao, go, mod(1, 2), mix_w_out[l].astype(BF16), ln_g[l, 1], ln_b[l, 1])

        x = _ffn_call(x, mod(2, 0), mod(2, 1), mod(2, 2), ffn2_w_in[l].astype(BF16),
                      ffn2_w_out[l].astype(BF16), ln_g[l, 2], ln_b[l, 2])
    return x
```

```python
import functools
import math

import numpy as np
import jax
import jax.numpy as jnp
from jax import lax
from jax.experimental import pallas as pl
from jax.experimental.pallas import tpu as pltpu

D_MODEL = 1024
DEPTH = 2
N_SUBLAYERS = 3
ATTN_WIDTH = 512
HGRN_WIDTH = 512
HEAD_DIM = 64
LANES = 128
N_SLABS = ATTN_WIDTH // LANES
ROPE_THETA = 500000.0
ROPE_DIM = HEAD_DIM // 4
ROPE_HALF = ROPE_DIM // 2
D_FF = 2816
FFN_RES_WEIGHT = 0.5
DEEPNORM_ALPHA = (2 * DEPTH) ** 0.25
LN_EPS = 1e-5
RMS_EPS = 1e-6
IN_COLS = 3 * ATTN_WIDTH + 4 * HGRN_WIDTH
BAND = 128
ATTN_BLOCK = 128
MID_DILATION = 4
MAX_DILATION = 16
ATTN_TILE = ATTN_BLOCK * MAX_DILATION
ATTN_ROW_CHUNK = 32
HGRN_CHUNK = 128
HGRN_LEVELS = 7
HGRN_TILE = 512
ROW_TILE = 512
FF_CHUNK = 256
ADA_COLS = N_SUBLAYERS * 3 * D_MODEL
NEG_BIG = -1e30
LOG2_E = math.log2(math.e)
VMEM_LIMIT = 56 * 1024 * 1024

F32 = jnp.float32
BF16 = jnp.bfloat16
NT_DIMS = (((1,), (1,)), ((), ()))
TN_DIMS = (((0,), (0,)), ((), ()))


def _sigmoid(x):
    return 1.0 / (1.0 + jnp.exp(-x))


def _layer_norm(z, g, b):
    mu = jnp.mean(z, axis=-1, keepdims=True)
    zc = z - mu
    var = jnp.mean(zc * zc, axis=-1, keepdims=True)
    return zc * lax.rsqrt(var + LN_EPS) * g + b


def _resident(shape):
    zeros = (0,) * len(shape)
    return pl.BlockSpec(shape, lambda *_: zeros, pipeline_mode=pl.Buffered(1))


def _ada_kernel(c_ref, w_ref, b_ref, o_ref):
    c = c_ref[...]
    cond = c * _sigmoid(c)
    o_ref[0] = jnp.dot(cond, w_ref[0], preferred_element_type=F32,
                       precision=lax.Precision.HIGHEST) + b_ref[0]


def _ada_call(c, ada_w, ada_b):
    B = c.shape[0]
    ncol = ADA_COLS // D_MODEL
    return pl.pallas_call(
        _ada_kernel,
        grid=(DEPTH, ncol),
        in_specs=[
            pl.BlockSpec((B, D_MODEL), lambda l, j: (0, 0)),
            pl.BlockSpec((1, D_MODEL, D_MODEL), lambda l, j: (l, 0, j)),
            pl.BlockSpec((1, 1, D_MODEL), lambda l, j: (l, 0, j)),
        ],
        out_specs=pl.BlockSpec((1, B, D_MODEL), lambda l, j: (l, 0, j)),
        out_shape=jax.ShapeDtypeStruct((DEPTH, B, ADA_COLS), F32),
        compiler_params=pltpu.CompilerParams(
            dimension_semantics=("parallel", "parallel"), vmem_limit_bytes=VMEM_LIMIT),
        name="ada",
    )(c, ada_w, ada_b.reshape(DEPTH, 1, ADA_COLS))


def _rope_kernel(pos_ref, cos_ref, sin_up_ref, sin_dn_ref):
    pos = pos_ref[0].astype(F32)
    lane = lax.broadcasted_iota(jnp.int32, (1, LANES), 1)
    in_head = lane & (HEAD_DIM - 1)
    freq = (in_head & (ROPE_HALF - 1)).astype(F32)
    inv_freq = jnp.exp(freq * (-math.log(ROPE_THETA) * 2.0 / ROPE_DIM))
    ang = pos * inv_freq
    cos = jnp.cos(ang)
    sin = jnp.sin(ang)
    first = in_head < ROPE_HALF
    second = jnp.logical_and(in_head >= ROPE_HALF, in_head < ROPE_DIM)
    cos_ref[0] = jnp.where(in_head < ROPE_DIM, cos, 1.0)
    sin_up_ref[0] = jnp.where(first, -sin, 0.0)
    sin_dn_ref[0] = jnp.where(second, sin, 0.0)


def _rope_call(positions):
    B, S = positions.shape
    rows = min(S, 1024)
    spec = pl.BlockSpec((1, rows, LANES), lambda b, i: (b, i, 0))
    shape = jax.ShapeDtypeStruct((B, S, LANES), F32)
    return pl.pallas_call(
        _rope_kernel,
        grid=(B, S // rows),
        in_specs=[pl.BlockSpec((1, rows, 1), lambda b, i: (b, i, 0))],
        out_specs=[spec, spec, spec],
        out_shape=[shape, shape, shape],
        compiler_params=pltpu.CompilerParams(
            dimension_semantics=("parallel", "parallel"), vmem_limit_bytes=VMEM_LIMIT),
        name="rope_tables",
    )(positions.reshape(B, S, 1))


def _ffn_kernel(x_ref, shift_ref, scale_ref, gate_ref, win_ref, wout_ref, g_ref, b_ref,
                o_ref, act_ref):
    x = x_ref[0]
    h = (x * (1.0 + scale_ref[0]) + shift_ref[0]).astype(BF16)
    for j in range(D_FF // FF_CHUNK):
        lo = j * FF_CHUNK
        g = jnp.dot(h, win_ref[:, lo:lo + FF_CHUNK], preferred_element_type=F32)
        u = jnp.dot(h, win_ref[:, D_FF + lo:D_FF + lo + FF_CHUNK], preferred_element_type=F32)
        act_ref[:, lo:lo + FF_CHUNK] = (g * _sigmoid(g) * u).astype(BF16)
    y = jnp.dot(act_ref[...], wout_ref[...], preferred_element_type=F32)
    z = DEEPNORM_ALPHA * x + (FFN_RES_WEIGHT * (1.0 + gate_ref[0])) * y
    o_ref[0] = _layer_norm(z, g_ref[...], b_ref[...])


def _ffn_call(x, shift, scale, gate, w_in, w_out, ln_g, ln_b):
    B, S, D = x.shape
    rows = min(S, ROW_TILE)
    tile = pl.BlockSpec((1, rows, D), lambda b, i: (b, i, 0))
    per_batch = pl.BlockSpec((1, 1, D), lambda b, i: (b, 0, 0))
    return pl.pallas_call(
        _ffn_kernel,
        grid=(B, S // rows),
        in_specs=[tile, per_batch, per_batch, per_batch,
                  _resident((D, 2 * D_FF)), _resident((D_FF, D)),
                  _resident((1, D)), _resident((1, D))],
        out_specs=tile,
        out_shape=jax.ShapeDtypeStruct((B, S, D), F32),
        scratch_shapes=[pltpu.VMEM((rows, D_FF), BF16)],
        compiler_params=pltpu.CompilerParams(
            dimension_semantics=("parallel", "parallel"), vmem_limit_bytes=VMEM_LIMIT),
        name="ffn",
    )(x, shift, scale, gate, w_in, w_out, ln_g.reshape(1, D), ln_b.reshape(1, D))


def _proj_kernel(x_ref, shift_ref, scale_ref, w_ref, cos_ref, sup_ref, sdn_ref,
                 q_ref, k_ref, v_ref, hq_ref, hf_ref, hi_ref, hg_ref):
    x = x_ref[0]
    h = (x * (1.0 + scale_ref[0]) + shift_ref[0]).astype(BF16)
    cos, sup, sdn = cos_ref[0], sup_ref[0], sdn_ref[0]
    outs = (q_ref, k_ref, v_ref, hq_ref, hf_ref, hi_ref, hg_ref)
    for part, out in enumerate(outs):
        p = jnp.dot(h, w_ref[:, part * ATTN_WIDTH:(part + 1) * ATTN_WIDTH],
                    preferred_element_type=F32)
        for s in range(N_SLABS):
            slab = p[:, s * LANES:(s + 1) * LANES]
            if part < 2:
                slab = (slab * cos + pltpu.roll(slab, LANES - ROPE_HALF, 1) * sup
                        + pltpu.roll(slab, ROPE_HALF, 1) * sdn)
                if part == 0:
                    slab = slab * (HEAD_DIM ** -0.5 * LOG2_E)
            out[0, s] = slab.astype(out.dtype)


def _proj_call(x, shift, scale, w, cos, sup, sdn):
    B, S, D = x.shape
    rows = min(S, ROW_TILE)
    tile = pl.BlockSpec((1, rows, D), lambda b, i: (b, i, 0))
    per_batch = pl.BlockSpec((1, 1, D), lambda b, i: (b, 0, 0))
    table = pl.BlockSpec((1, rows, LANES), lambda b, i: (b, i, 0))
    out_spec = pl.BlockSpec((1, N_SLABS, rows, LANES), lambda b, i: (b, 0, i, 0))
    dtypes = (F32, F32, F32, BF16, F32, BF16, BF16)
    return pl.pallas_call(
        _proj_kernel,
        grid=(B, S // rows),
        in_specs=[tile, per_batch, per_batch, _resident((D, IN_COLS)), table, table, table],
        out_specs=[out_spec] * 7,
        out_shape=[jax.ShapeDtypeStruct((B, N_SLABS, S, LANES), dt) for dt in dtypes],
        compiler_params=pltpu.CompilerParams(
            dimension_semantics=("parallel", "parallel"), vmem_limit_bytes=VMEM_LIMIT),
        name="mix_proj",
    )(x, shift, scale, w, cos, sup, sdn)


def _attn_kernel(q_ref, kc_ref, kp_ref, vc_ref, vp_ref, o_ref,
                 bias_ref, q4_ref, k4_ref, v4_ref, m_ref, l_ref, acc_ref,
                 s_ref, p_ref, mb_ref, lb_ref):
    first_tile = pl.program_id(2) == 0
    lane = lax.broadcasted_iota(jnp.int32, (1, LANES), 1)
    head0 = lane < HEAD_DIM
    n4 = ATTN_TILE // MID_DILATION

    row = lax.broadcasted_iota(jnp.int32, (2 * ATTN_BLOCK, 2 * ATTN_BLOCK), 0) & (ATTN_BLOCK - 1)
    col = lax.broadcasted_iota(jnp.int32, (2 * ATTN_BLOCK, 2 * ATTN_BLOCK), 1)
    visible = jnp.logical_and(col >= row, col <= row + BAND)
    bias_ref[0] = jnp.where(visible, 0.0, NEG_BIG)
    no_prev = jnp.logical_and(visible, col >= jnp.where(first_tile, ATTN_BLOCK, 0))
    bias_ref[1] = jnp.where(no_prev, 0.0, NEG_BIG)

    for r in range(MID_DILATION):
        q4_ref[r] = q_ref[0, 0, pl.ds(r, n4, stride=MID_DILATION), :]
        k4_ref[r, :n4] = kp_ref[0, 0, pl.ds(r, n4, stride=MID_DILATION), :]
        k4_ref[r, n4:] = kc_ref[0, 0, pl.ds(r, n4, stride=MID_DILATION), :]
        v4_ref[r, :n4] = vp_ref[0, 0, pl.ds(r, n4, stride=MID_DILATION), :]
        v4_ref[r, n4:] = vc_ref[0, 0, pl.ds(r, n4, stride=MID_DILATION), :]

    def block_stats(q, k2, v2, which_bias):
        zero = jnp.zeros_like(q)
        qq = jnp.concatenate([jnp.where(head0, q, zero), jnp.where(head0, zero, q)], axis=0)
        s_ref[...] = lax.dot_general(qq.astype(BF16), k2.astype(BF16), NT_DIMS,
                                     preferred_element_type=F32) + bias_ref[which_bias]
        for c in range(2 * ATTN_BLOCK // ATTN_ROW_CHUNK):
            rows = slice(c * ATTN_ROW_CHUNK, (c + 1) * ATTN_ROW_CHUNK)
            s = s_ref[rows, :]
            m = jnp.max(s, axis=-1, keepdims=True)
            p = jnp.exp2(s - m)
            p_ref[rows, :] = p.astype(BF16)
            mb_ref[rows, :] = jnp.broadcast_to(m, (ATTN_ROW_CHUNK, LANES))
            lb_ref[rows, :] = jnp.broadcast_to(jnp.sum(p, axis=-1, keepdims=True),
                                               (ATTN_ROW_CHUNK, LANES))
        pv = jnp.dot(p_ref[...], v2.astype(BF16), preferred_element_type=F32)
        acc_b = jnp.where(head0, pv[:ATTN_BLOCK], pv[ATTN_BLOCK:])
        m_b = jnp.where(head0, mb_ref[:ATTN_BLOCK, :], mb_ref[ATTN_BLOCK:, :])
        l_b = jnp.where(head0, lb_ref[:ATTN_BLOCK, :], lb_ref[ATTN_BLOCK:, :])
        return m_b, l_b, acc_b

    def merged(rows, m_b, l_b, acc_b):
        m_old = m_ref[rows, :]
        m_new = jnp.maximum(m_old, m_b)
        w_old = jnp.exp2(m_old - m_new)
        w_blk = jnp.exp2(m_b - m_new)
        return (m_new, w_old * l_ref[rows, :] + w_blk * l_b, w_old * acc_ref[rows, :] + w_blk * acc_b)

    for r16 in range(MAX_DILATION):
        r4, j = r16 % MID_DILATION, r16 // MID_DILATION
        q = q4_ref[r4, pl.ds(j, ATTN_BLOCK, stride=MID_DILATION), :]
        k2 = k4_ref[r4, pl.ds(j, 2 * ATTN_BLOCK, stride=MID_DILATION), :]
        v2 = v4_ref[r4, pl.ds(j, 2 * ATTN_BLOCK, stride=MID_DILATION), :]
        m_b, l_b, acc_b = block_stats(q, k2, v2, 1)
        rows = pl.ds(r16, ATTN_BLOCK, stride=MAX_DILATION)
        m_ref[rows, :] = m_b
        l_ref[rows, :] = l_b
        acc_ref[rows, :] = acc_b

    for r in range(MID_DILATION):
        for nb in range(n4 // ATTN_BLOCK):
            lo = nb * ATTN_BLOCK
            q = q4_ref[r, lo:lo + ATTN_BLOCK, :]
            k2 = k4_ref[r, n4 - ATTN_BLOCK + lo:n4 + ATTN_BLOCK + lo, :]
            v2 = v4_ref[r, n4 - ATTN_BLOCK + lo:n4 + ATTN_BLOCK + lo, :]
            m_b, l_b, acc_b = block_stats(q, k2, v2, 1 if nb == 0 else 0)
            rows = pl.ds(r + MID_DILATION * lo, ATTN_BLOCK, stride=MID_DILATION)
            m_new, l_new, acc_new = merged(rows, m_b, l_b, acc_b)
            m_ref[rows, :] = m_new
            l_ref[rows, :] = l_new
            acc_ref[rows, :] = acc_new

    for nb in range(ATTN_TILE // ATTN_BLOCK):
        lo = nb * ATTN_BLOCK
        q = q_ref[0, 0, lo:lo + ATTN_BLOCK, :]
        if nb == 0:
            k2 = jnp.concatenate([kp_ref[0, 0, ATTN_TILE - ATTN_BLOCK:, :], kc_ref[0, 0, :ATTN_BLOCK, :]], axis=0)
            v2 = jnp.concatenate([vp_ref[0, 0, ATTN_TILE - ATTN_BLOCK:, :], vc_ref[0, 0, :ATTN_BLOCK, :]], axis=0)
        else:
            k2 = kc_ref[0, 0, lo - ATTN_BLOCK:lo + ATTN_BLOCK, :]
            v2 = vc_ref[0, 0, lo - ATTN_BLOCK:lo + ATTN_BLOCK, :]
        m_b, l_b, acc_b = block_stats(q, k2, v2, 1 if nb == 0 else 0)
        _, l_new, acc_new = merged(pl.ds(lo, ATTN_BLOCK), m_b, l_b, acc_b)
        o_ref[0, lo:lo + ATTN_BLOCK, :] = (acc_new / l_new).astype(o_ref.dtype)


def _attn_call(q, k, v):
    B, _, S, _ = q.shape
    cur = pl.BlockSpec((1, 1, ATTN_TILE, LANES), lambda b, s, i: (b, s, i, 0))
    prev = pl.BlockSpec((1, 1, ATTN_TILE, LANES), lambda b, s, i: (b, s, jnp.maximum(i - 1, 0), 0))
    n4 = ATTN_TILE // MID_DILATION
    return pl.pallas_call(
        _attn_kernel,
        grid=(B, N_SLABS, S // ATTN_TILE),
        in_specs=[cur, cur, prev, cur, prev],
        out_specs=pl.BlockSpec((1, ATTN_TILE, LANES), lambda b, s, i: (b, i, s)),
        out_shape=jax.ShapeDtypeStruct((B, S, ATTN_WIDTH), BF16),
        scratch_shapes=[pltpu.VMEM((2, 2 * ATTN_BLOCK, 2 * ATTN_BLOCK), F32),
                        pltpu.VMEM((MID_DILATION, n4, LANES), F32),
                        pltpu.VMEM((MID_DILATION, 2 * n4, LANES), F32),
                        pltpu.VMEM((MID_DILATION, 2 * n4, LANES), F32),
                        pltpu.VMEM((ATTN_TILE, LANES), F32),
                        pltpu.VMEM((ATTN_TILE, LANES), F32),
                        pltpu.VMEM((ATTN_TILE, LANES), F32),
                        pltpu.VMEM((2 * ATTN_BLOCK, 2 * ATTN_BLOCK), F32),
                        pltpu.VMEM((2 * ATTN_BLOCK, 2 * ATTN_BLOCK), BF16),
                        pltpu.VMEM((2 * ATTN_BLOCK, LANES), F32),
                        pltpu.VMEM((2 * ATTN_BLOCK, LANES), F32)],
        compiler_params=pltpu.CompilerParams(
            dimension_semantics=("parallel", "parallel", "parallel"), vmem_limit_bytes=VMEM_LIMIT),
        name="dilated_attn",
    )(q, k, k, v, v)


def _hgrn_scan_matrices():
    C = HGRN_CHUNK
    t = np.arange(C)[:, None]
    u = np.arange(C)[None, :]
    mats = []
    for lvl in range(HGRN_LEVELS):
        start = (t >> lvl) << lvl
        end = start + (1 << lvl) - 1
        is_query = ((t >> lvl) & 1) == 1
        mats.append(np.where(is_query, (u >= start) & (u <= t), (u > t) & (u <= end)))
    mats.append(np.broadcast_to(u <= t, (C, C)))
    mats.append(np.broadcast_to(u > t, (C, C)))
    return np.concatenate(mats, axis=0).astype(np.float32)


def _hgrn_kernel(hq_ref, hf_ref, hi_ref, hg_ref, loglb_ref, log1mlb_ref, onemlb_ref, nw_ref,
                 scan_ref, o_ref, state_ref):
    C = HGRN_CHUNK

    @pl.when(pl.program_id(2) == 0)
    def _():
        state_ref[...] = jnp.zeros_like(state_ref)

    log_lb, log1m_lb, one_m_lb, norm_w = loglb_ref[0], log1mlb_ref[0], onemlb_ref[0], nw_ref[0]

    t_idx = lax.broadcasted_iota(jnp.int32, (C, C), 0)
    s_idx = lax.broadcasted_iota(jnp.int32, (C, C), 1)
    differ = t_idx ^ s_idx
    level = jnp.full((C, C), -1, jnp.int32)
    for lvl in range(HGRN_LEVELS):
        level = level + ((differ >> lvl) > 0).astype(jnp.int32)
    level = jnp.where(t_idx < s_idx, -2, level)

    def nt_dot(a, b):
        return lax.dot_general(a, b, NT_DIMS, preferred_element_type=F32)

    for c in range(hq_ref.shape[2] // C):
        rows = slice(c * C, (c + 1) * C)
        fp = hf_ref[0, 0, rows, :]
        hq = hq_ref[0, 0, rows, :].astype(F32)
        hg = hg_ref[0, 0, rows, :].astype(F32)
        v = hi_ref[0, 0, rows, :]

        log_sig = jnp.minimum(fp, 0.0) - jnp.log(1.0 + jnp.exp(-jnp.abs(fp)))
        l2 = log1m_lb + log_sig
        a = jnp.maximum(log_lb, l2) + jnp.log(1.0 + jnp.exp(-jnp.abs(log_lb - l2)))
        k = one_m_lb * jnp.exp(log_sig - fp)
        q = hq * _sigmoid(hq)

        a_hi = a.astype(BF16)
        a_lo = (a - a_hi.astype(F32)).astype(BF16)
        seg = jnp.dot(scan_ref[...], jnp.concatenate([a_hi, a_lo], axis=1),
                      preferred_element_type=F32)
        seg = seg[:, :LANES] + seg[:, LANES:]

        attn = jnp.where(level == -1, nt_dot(q.astype(BF16), k.astype(BF16)), 0.0)
        for lvl in range(HGRN_LEVELS):
            factor = jnp.exp(seg[lvl * C:(lvl + 1) * C])
            scores = nt_dot((q * factor).astype(BF16), (k * factor).astype(BF16))
            attn = jnp.where(level == lvl, scores, attn)

        state = state_ref[...]
        from_start = seg[HGRN_LEVELS * C:(HGRN_LEVELS + 1) * C]
        to_end = seg[(HGRN_LEVELS + 1) * C:(HGRN_LEVELS + 2) * C]
        o = (jnp.dot(attn.astype(BF16), v, preferred_element_type=F32)
             + nt_dot((q * jnp.exp(from_start)).astype(BF16), state.astype(BF16)))
        k_out = (k * jnp.exp(to_end)).astype(BF16)
        chunk_decay = jnp.exp(from_start[C - 1:C, :])
        state_ref[...] = state * chunk_decay + lax.dot_general(
            v, k_out, TN_DIMS, preferred_element_type=F32)

        o = o * lax.rsqrt(jnp.mean(o * o, axis=-1, keepdims=True) + RMS_EPS)
        o = o * norm_w * (hg * _sigmoid(hg))
        o_ref[0, rows, :] = o.astype(o_ref.dtype)


def _hgrn_call(hq, hf, hi, hg, log_lb, log1m_lb, one_m_lb, norm_w):
    B, H, S, _ = hq.shape
    rows = min(S, HGRN_TILE)
    tile = pl.BlockSpec((1, 1, rows, LANES), lambda b, h, i: (b, h, i, 0))
    per_head = pl.BlockSpec((1, 1, LANES), lambda b, h, i: (h, 0, 0))
    scan = jnp.asarray(_hgrn_scan_matrices(), dtype=BF16)
    return pl.pallas_call(
        _hgrn_kernel,
        grid=(B, H, S // rows),
        in_specs=[tile, tile, tile, tile, per_head, per_head, per_head, per_head,
                  _resident(scan.shape)],
        out_specs=pl.BlockSpec((1, rows, LANES), lambda b, h, i: (b, i, h)),
        out_shape=jax.ShapeDtypeStruct((B, S, HGRN_WIDTH), BF16),
        scratch_shapes=[pltpu.VMEM((LANES, LANES), F32)],
        compiler_params=pltpu.CompilerParams(
            dimension_semantics=("parallel", "parallel", "arbitrary"), vmem_limit_bytes=VMEM_LIMIT),
        name="hgrn2",
    )(hq, hf, hi, hg, log_lb, log1m_lb, one_m_lb, norm_w, scan)


def _mix_out_kernel(x_ref, ao_ref, go_ref, gate_ref, w_ref, g_ref, b_ref, o_ref):
    x = x_ref[0]
    y = (jnp.dot(ao_ref[0], w_ref[:ATTN_WIDTH, :], preferred_element_type=F32)
         + jnp.dot(go_ref[0], w_ref[ATTN_WIDTH:, :], preferred_element_type=F32))
    z = DEEPNORM_ALPHA * x + (1.0 + gate_ref[0]) * y
    o_ref[0] = _layer_norm(z, g_ref[...], b_ref[...])


def _mix_out_call(x, ao, go, gate, w_out, ln_g, ln_b):
    B, S, D = x.shape
    rows = min(S, ROW_TILE)
    tile = pl.BlockSpec((1, rows, D), lambda b, i: (b, i, 0))
    half = pl.BlockSpec((1, rows, ATTN_WIDTH), lambda b, i: (b, i, 0))
    per_batch = pl.BlockSpec((1, 1, D), lambda b, i: (b, 0, 0))
    return pl.pallas_call(
        _mix_out_kernel,
        grid=(B, S // rows),
        in_specs=[tile, half, half, per_batch, _resident((D, D)), _resident((1, D)), _resident((1, D))],
        out_specs=tile,
        out_shape=jax.ShapeDtypeStruct((B, S, D), F32),
        compiler_params=pltpu.CompilerParams(
            dimension_semantics=("parallel", "parallel"), vmem_limit_bytes=VMEM_LIMIT),
        name="mix_out",
    )(x, ao, go, gate, w_out, ln_g.reshape(1, D), ln_b.reshape(1, D))


def kernel(x, c, positions, ln_g, ln_b, ada_w, ada_b, ffn1_w_in, ffn1_w_out, ffn2_w_in, ffn2_w_out,
           mix_w_in, mix_w_out, hgrn_norm_w, hgrn_lb_logits):
    B, S, D = x.shape
    assert D == D_MODEL and S % ATTN_TILE == 0 and S % HGRN_TILE == 0

    lb_all = jnp.cumsum(jax.nn.softmax(hgrn_lb_logits.astype(F32), axis=0), axis=0)
    lb_all = (lb_all - lb_all[0:1]).reshape(DEPTH, N_SLABS, 1, LANES)
    log_lb, log1m_lb, one_m_lb = jnp.log(lb_all), jnp.log1p(-lb_all), 1.0 - lb_all
    norm_w = hgrn_norm_w.astype(F32).reshape(DEPTH, N_SLABS, 1, LANES)

    ada = _ada_call(c, ada_w, ada_b).reshape(DEPTH, B, N_SUBLAYERS, 3, 1, D)
    cos, sin_up, sin_dn = _rope_call(positions)

    for l in range(DEPTH):
        def mod(sub, kind, l=l):
            return ada[l, :, sub, kind]

        x = _ffn_call(x, mod(0, 0), mod(0, 1), mod(0, 2), ffn1_w_in[l].astype(BF16),
                      ffn1_w_out[l].astype(BF16), ln_g[l, 0], ln_b[l, 0])

        q, k, v, hq, hf, hi, hg = _proj_call(x, mod(1, 0), mod(1, 1), mix_w_in[l].astype(BF16),
                                             cos, sin_up, sin_dn)
        ao = _attn_call(q, k, v)
        go = _hgrn_call(hq, hf, hi, hg, log_lb[l], log1m_lb[l], one_m_lb[l], norm_w[l])
        x = _mix_out_call(x, ao, go, mod(1, 2), mix_w_out[l].astype(BF16), ln_g[l, 1], ln_b[l, 1])

        x = _ffn_call(x, mod(2, 0), mod(2, 1), mod(2, 2), ffn2_w_in[l].astype(BF16),
                      ffn2_w_out[l].astype(BF16), ln_g[l, 2], ln_b[l, 2])
    return x
```

```python
import functools
import math

import numpy as np
import jax
import jax.numpy as jnp
from jax import lax
from jax.experimental import pallas as pl
from jax.experimental.pallas import tpu as pltpu

D_MODEL = 1024
DEPTH = 2
N_SUBLAYERS = 3
ATTN_WIDTH = 512
HGRN_WIDTH = 512
HEAD_DIM = 64
LANES = 128
N_SLABS = ATTN_WIDTH // LANES
ROPE_THETA = 500000.0
ROPE_DIM = HEAD_DIM // 4
ROPE_HALF = ROPE_DIM // 2
D_FF = 2816
FFN_RES_WEIGHT = 0.5
DEEPNORM_ALPHA = (2 * DEPTH) ** 0.25
LN_EPS = 1e-5
RMS_EPS = 1e-6
IN_COLS = 3 * ATTN_WIDTH + 4 * HGRN_WIDTH
BAND = 128
ATTN_BLOCK = 128
MID_DILATION = 4
MAX_DILATION = 16
ATTN_TILE = ATTN_BLOCK * MAX_DILATION
ATTN_ROW_CHUNK = 32
HGRN_CHUNK = 128
HGRN_LEVELS = 7
HGRN_TILE = 512
ROW_TILE = 512
FFN_GROUPS = 2
HGRN_HEADS_PER_STEP = 2
FF_CHUNK = 256
ADA_COLS = N_SUBLAYERS * 3 * D_MODEL
NEG_BIG = -1e30
LOG2_E = math.log2(math.e)
VMEM_LIMIT = 56 * 1024 * 1024

F32 = jnp.float32
BF16 = jnp.bfloat16
NT_DIMS = (((1,), (1,)), ((), ()))
TN_DIMS = (((0,), (0,)), ((), ()))


def _sigmoid(x):
    return 1.0 / (1.0 + jnp.exp(-x))


def _layer_norm(z, g, b):
    mu = jnp.mean(z, axis=-1, keepdims=True)
    zc = z - mu
    var = jnp.mean(zc * zc, axis=-1, keepdims=True)
    return zc * lax.rsqrt(var + LN_EPS) * g + b


def _resident(shape):
    zeros = (0,) * len(shape)
    return pl.BlockSpec(shape, lambda *_: zeros, pipeline_mode=pl.Buffered(1))


def _ada_kernel(c_ref, w_ref, b_ref, o_ref):
    c = c_ref[...]
    cond = c * _sigmoid(c)
    o_ref[0] = jnp.dot(cond, w_ref[0], preferred_element_type=F32,
                       precision=lax.Precision.HIGHEST) + b_ref[0]


def _ada_call(c, ada_w, ada_b):
    B = c.shape[0]
    ncol = ADA_COLS // D_MODEL
    return pl.pallas_call(
        _ada_kernel,
        grid=(DEPTH, ncol),
        in_specs=[
            pl.BlockSpec((B, D_MODEL), lambda l, j: (0, 0)),
            pl.BlockSpec((1, D_MODEL, D_MODEL), lambda l, j: (l, 0, j)),
            pl.BlockSpec((1, 1, D_MODEL), lambda l, j: (l, 0, j)),
        ],
        out_specs=pl.BlockSpec((1, B, D_MODEL), lambda l, j: (l, 0, j)),
        out_shape=jax.ShapeDtypeStruct((DEPTH, B, ADA_COLS), F32),
        compiler_params=pltpu.CompilerParams(
            dimension_semantics=("parallel", "parallel"), vmem_limit_bytes=VMEM_LIMIT),
        name="ada",
    )(c, ada_w, ada_b.reshape(DEPTH, 1, ADA_COLS))


def _rope_kernel(pos_ref, cos_ref, sin_up_ref, sin_dn_ref):
    pos = pos_ref[0].astype(F32)
    lane = lax.broadcasted_iota(jnp.int32, (1, LANES), 1)
    in_head = lane & (HEAD_DIM - 1)
    freq = (in_head & (ROPE_HALF - 1)).astype(F32)
    inv_freq = jnp.exp(freq * (-math.log(ROPE_THETA) * 2.0 / ROPE_DIM))
    ang = pos * inv_freq
    cos = jnp.cos(ang)
    sin = jnp.sin(ang)
    first = in_head < ROPE_HALF
    second = jnp.logical_and(in_head >= ROPE_HALF, in_head < ROPE_DIM)
    cos_ref[0] = jnp.where(in_head < ROPE_DIM, cos, 1.0)
    sin_up_ref[0] = jnp.where(first, -sin, 0.0)
    sin_dn_ref[0] = jnp.where(second, sin, 0.0)


def _rope_call(positions):
    B, S = positions.shape
    rows = min(S, 1024)
    spec = pl.BlockSpec((1, rows, LANES), lambda b, i: (b, i, 0))
    shape = jax.ShapeDtypeStruct((B, S, LANES), F32)
    return pl.pallas_call(
        _rope_kernel,
        grid=(B, S // rows),
        in_specs=[pl.BlockSpec((1, rows, 1), lambda b, i: (b, i, 0))],
        out_specs=[spec, spec, spec],
        out_shape=[shape, shape, shape],
        compiler_params=pltpu.CompilerParams(
            dimension_semantics=("parallel", "parallel"), vmem_limit_bytes=VMEM_LIMIT),
        name="rope_tables",
    )(positions.reshape(B, S, 1))


def _ffn_rows(x, rows, shift_ref, scale_ref, gate_ref, win_ref, wout_ref, g_ref, b_ref, act_ref):
    h = (x * (1.0 + scale_ref[0]) + shift_ref[0]).astype(BF16)
    for j in range(D_FF // FF_CHUNK):
        lo = j * FF_CHUNK
        g = jnp.dot(h, win_ref[:, lo:lo + FF_CHUNK], preferred_element_type=F32)
        u = jnp.dot(h, win_ref[:, D_FF + lo:D_FF + lo + FF_CHUNK], preferred_element_type=F32)
        act_ref[rows, lo:lo + FF_CHUNK] = (g * _sigmoid(g) * u).astype(BF16)
    y = jnp.dot(act_ref[rows, :], wout_ref[...], preferred_element_type=F32)
    z = DEEPNORM_ALPHA * x + (FFN_RES_WEIGHT * (1.0 + gate_ref[0])) * y
    return _layer_norm(z, g_ref[...], b_ref[...])


def _row_groups(ref):
    return [slice(r * ROW_TILE, (r + 1) * ROW_TILE) for r in range(ref.shape[1] // ROW_TILE)]


def _ffn_kernel(x_ref, shift_ref, scale_ref, gate_ref, win_ref, wout_ref, g_ref, b_ref,
                o_ref, act_ref):
    for rows in _row_groups(x_ref):
        o_ref[0, rows, :] = _ffn_rows(x_ref[0, rows, :], rows, shift_ref, scale_ref, gate_ref,
                                      win_ref, wout_ref, g_ref, b_ref, act_ref)


def _mix_out_ffn_kernel(x_ref, ao_ref, go_ref, mgate_ref, wmix_ref, mg_ref, mb_ref,
                        shift_ref, scale_ref, gate_ref, win_ref, wout_ref, g_ref, b_ref,
                        o_ref, act_ref):
    for rows in _row_groups(x_ref):
        y = (jnp.dot(ao_ref[0, rows, :], wmix_ref[:ATTN_WIDTH, :], preferred_element_type=F32)
             + jnp.dot(go_ref[0, rows, :], wmix_ref[ATTN_WIDTH:, :], preferred_element_type=F32))
        z = DEEPNORM_ALPHA * x_ref[0, rows, :] + (1.0 + mgate_ref[0]) * y
        x_mid = _layer_norm(z, mg_ref[...], mb_ref[...])
        o_ref[0, rows, :] = _ffn_rows(x_mid, rows, shift_ref, scale_ref, gate_ref,
                                      win_ref, wout_ref, g_ref, b_ref, act_ref)


def _ffn_specs(D):
    per_batch = pl.BlockSpec((1, 1, D), lambda b, i: (b, 0, 0))
    return [per_batch, per_batch, per_batch, _resident((D, 2 * D_FF)), _resident((D_FF, D)),
            _resident((1, D)), _resident((1, D))]


def _ffn_call(x, shift, scale, gate, w_in, w_out, ln_g, ln_b):
    B, S, D = x.shape
    rows = FFN_GROUPS * ROW_TILE
    tile = pl.BlockSpec((1, rows, D), lambda b, i: (b, i, 0))
    return pl.pallas_call(
        _ffn_kernel,
        grid=(B, S // rows),
        in_specs=[tile] + _ffn_specs(D),
        out_specs=tile,
        out_shape=jax.ShapeDtypeStruct((B, S, D), F32),
        scratch_shapes=[pltpu.VMEM((rows, D_FF), BF16)],
        compiler_params=pltpu.CompilerParams(
            dimension_semantics=("parallel", "parallel"), vmem_limit_bytes=VMEM_LIMIT),
        name="ffn",
    )(x, shift, scale, gate, w_in, w_out, ln_g.reshape(1, D), ln_b.reshape(1, D))


def _mix_out_ffn_call(x, ao, go, mix_gate, w_mix, mix_ln_g, mix_ln_b,
                      shift, scale, gate, w_in, w_out, ln_g, ln_b):
    B, S, D = x.shape
    rows = FFN_GROUPS * ROW_TILE
    tile = pl.BlockSpec((1, rows, D), lambda b, i: (b, i, 0))
    half = pl.BlockSpec((1, rows, ATTN_WIDTH), lambda b, i: (b, i, 0))
    per_batch = pl.BlockSpec((1, 1, D), lambda b, i: (b, 0, 0))
    return pl.pallas_call(
        _mix_out_ffn_kernel,
        grid=(B, S // rows),
        in_specs=[tile, half, half, per_batch, _resident((D, D)), _resident((1, D)), _resident((1, D))]
                 + _ffn_specs(D),
        out_specs=tile,
        out_shape=jax.ShapeDtypeStruct((B, S, D), F32),
        scratch_shapes=[pltpu.VMEM((rows, D_FF), BF16)],
        compiler_params=pltpu.CompilerParams(
            dimension_semantics=("parallel", "parallel"), vmem_limit_bytes=VMEM_LIMIT),
        name="mix_out_ffn",
    )(x, ao, go, mix_gate, w_mix, mix_ln_g.reshape(1, D), mix_ln_b.reshape(1, D),
      shift, scale, gate, w_in, w_out, ln_g.reshape(1, D), ln_b.reshape(1, D))


def _proj_kernel(x_ref, shift_ref, scale_ref, w_ref, cos_ref, sup_ref, sdn_ref,
                 q_ref, k_ref, v_ref, hq_ref, hf_ref, hi_ref, hg_ref):
    x = x_ref[0]
    h = (x * (1.0 + scale_ref[0]) + shift_ref[0]).astype(BF16)
    cos, sup, sdn = cos_ref[0], sup_ref[0], sdn_ref[0]
    outs = (q_ref, k_ref, v_ref, hq_ref, hf_ref, hi_ref, hg_ref)
    for part, out in enumerate(outs):
        p = jnp.dot(h, w_ref[:, part * ATTN_WIDTH:(part + 1) * ATTN_WIDTH],
                    preferred_element_type=F32)
        for s in range(N_SLABS):
            slab = p[:, s * LANES:(s + 1) * LANES]
            if part < 2:
                slab = (slab * cos + pltpu.roll(slab, LANES - ROPE_HALF, 1) * sup
                        + pltpu.roll(slab, ROPE_HALF, 1) * sdn)
                if part == 0:
                    slab = slab * (HEAD_DIM ** -0.5 * LOG2_E)
            out[0, s] = slab.astype(out.dtype)


def _proj_call(x, shift, scale, w, cos, sup, sdn):
    B, S, D = x.shape
    rows = min(S, ROW_TILE)
    tile = pl.BlockSpec((1, rows, D), lambda b, i: (b, i, 0))
    per_batch = pl.BlockSpec((1, 1, D), lambda b, i: (b, 0, 0))
    table = pl.BlockSpec((1, rows, LANES), lambda b, i: (b, i, 0))
    out_spec = pl.BlockSpec((1, N_SLABS, rows, LANES), lambda b, i: (b, 0, i, 0))
    dtypes = (F32, F32, F32, BF16, F32, BF16, BF16)
    return pl.pallas_call(
        _proj_kernel,
        grid=(B, S // rows),
        in_specs=[tile, per_batch, per_batch, _resident((D, IN_COLS)), table, table, table],
        out_specs=[out_spec] * 7,
        out_shape=[jax.ShapeDtypeStruct((B, N_SLABS, S, LANES), dt) for dt in dtypes],
        compiler_params=pltpu.CompilerParams(
            dimension_semantics=("parallel", "parallel"), vmem_limit_bytes=VMEM_LIMIT),
        name="mix_proj",
    )(x, shift, scale, w, cos, sup, sdn)


def _attn_kernel(q_ref, kc_ref, kp_ref, vc_ref, vp_ref, o_ref,
                 bias_ref, q4_ref, k4_ref, v4_ref, m_ref, l_ref, acc_ref, s_ref, p_ref):
    first_tile = pl.program_id(2) == 0
    lane = lax.broadcasted_iota(jnp.int32, (1, LANES), 1)
    head0 = lane < HEAD_DIM
    n4 = ATTN_TILE // MID_DILATION

    row = lax.broadcasted_iota(jnp.int32, (ATTN_BLOCK, 2 * ATTN_BLOCK), 0)
    col = lax.broadcasted_iota(jnp.int32, (ATTN_BLOCK, 2 * ATTN_BLOCK), 1)
    visible = jnp.logical_and(col >= row, col <= row + BAND)
    bias_ref[0] = jnp.where(visible, 0.0, NEG_BIG)
    no_prev = jnp.logical_and(visible, col >= jnp.where(first_tile, ATTN_BLOCK, 0))
    bias_ref[1] = jnp.where(no_prev, 0.0, NEG_BIG)

    for r in range(MID_DILATION):
        q4_ref[r] = q_ref[0, 0, pl.ds(r, n4, stride=MID_DILATION), :]
        k4_ref[r, :n4] = kp_ref[0, 0, pl.ds(r, n4, stride=MID_DILATION), :]
        k4_ref[r, n4:] = kc_ref[0, 0, pl.ds(r, n4, stride=MID_DILATION), :]
        v4_ref[r, :n4] = vp_ref[0, 0, pl.ds(r, n4, stride=MID_DILATION), :]
        v4_ref[r, n4:] = vc_ref[0, 0, pl.ds(r, n4, stride=MID_DILATION), :]

    n_chunks = ATTN_BLOCK // ATTN_ROW_CHUNK

    def score_stage(slot, q, k2, which_bias):
        zero = jnp.zeros_like(q)
        qq = jnp.concatenate([jnp.where(head0, q, zero), jnp.where(head0, zero, q)], axis=0)
        scores = lax.dot_general(qq.astype(BF16), k2.astype(BF16), NT_DIMS, preferred_element_type=F32)
        row_max = [[None] * n_chunks, [None] * n_chunks]
        for head in range(2):
            for c in range(n_chunks):
                lo = c * ATTN_ROW_CHUNK
                rows = slice(head * ATTN_BLOCK + lo, head * ATTN_BLOCK + lo + ATTN_ROW_CHUNK)
                s = scores[rows, :] + bias_ref[which_bias, lo:lo + ATTN_ROW_CHUNK, :]
                s_ref[slot, rows, :] = s
                row_max[head][c] = jnp.max(s, axis=-1, keepdims=True)
        return row_max

    def value_stage(slot, row_max, v2):
        for head in range(2):
            for c in range(n_chunks):
                lo = head * ATTN_BLOCK + c * ATTN_ROW_CHUNK
                p_ref[lo:lo + ATTN_ROW_CHUNK, :] = jnp.exp2(
                    s_ref[slot, lo:lo + ATTN_ROW_CHUNK, :] - row_max[head][c]).astype(BF16)
        v_ones = jnp.concatenate([v2.astype(BF16), jnp.ones((2 * ATTN_BLOCK, LANES), BF16)], axis=1)
        pv = jnp.dot(p_ref[...], v_ones, preferred_element_type=F32)
        acc_b = jnp.where(head0, pv[:ATTN_BLOCK, :LANES], pv[ATTN_BLOCK:, :LANES])
        l_b = jnp.where(head0, pv[:ATTN_BLOCK, LANES:], pv[ATTN_BLOCK:, LANES:])
        m_b = jnp.concatenate([jnp.where(head0, row_max[0][c], row_max[1][c]) for c in range(n_chunks)],
                              axis=0)
        return m_b, l_b, acc_b

    def merged(rows, m_b, l_b, acc_b):
        m_old = m_ref[rows, :]
        m_new = jnp.maximum(m_old, m_b)
        w_old = jnp.exp2(m_old - m_new)
        w_blk = jnp.exp2(m_b - m_new)
        return (m_new, w_old * l_ref[rows, :] + w_blk * l_b, w_old * acc_ref[rows, :] + w_blk * acc_b)

    blocks = []

    for r16 in range(MAX_DILATION):
        r4, j = r16 % MID_DILATION, r16 // MID_DILATION

        def load_qk(r4=r4, j=j):
            return (q4_ref[r4, pl.ds(j, ATTN_BLOCK, stride=MID_DILATION), :],
                    k4_ref[r4, pl.ds(j, 2 * ATTN_BLOCK, stride=MID_DILATION), :])

        def load_v(r4=r4, j=j):
            return v4_ref[r4, pl.ds(j, 2 * ATTN_BLOCK, stride=MID_DILATION), :]

        def consume(m_b, l_b, acc_b, r16=r16):
            rows = pl.ds(r16, ATTN_BLOCK, stride=MAX_DILATION)
            m_ref[rows, :] = m_b
            l_ref[rows, :] = l_b
            acc_ref[rows, :] = acc_b

        blocks.append((load_qk, 1, load_v, consume))

    for r in range(MID_DILATION):
        for nb in range(n4 // ATTN_BLOCK):
            lo = nb * ATTN_BLOCK
            keys = slice(n4 - ATTN_BLOCK + lo, n4 + ATTN_BLOCK + lo)

            def load_qk(r=r, lo=lo, keys=keys):
                return q4_ref[r, lo:lo + ATTN_BLOCK, :], k4_ref[r, keys, :]

            def load_v(r=r, keys=keys):
                return v4_ref[r, keys, :]

            def consume(m_b, l_b, acc_b, r=r, lo=lo):
                rows = pl.ds(r + MID_DILATION * lo, ATTN_BLOCK, stride=MID_DILATION)
                m_new, l_new, acc_new = merged(rows, m_b, l_b, acc_b)
                m_ref[rows, :] = m_new
                l_ref[rows, :] = l_new
                acc_ref[rows, :] = acc_new

            blocks.append((load_qk, 1 if nb == 0 else 0, load_v, consume))

    for nb in range(ATTN_TILE // ATTN_BLOCK):
        lo = nb * ATTN_BLOCK

        def with_prev(cur_ref, prev_ref, lo=lo):
            if lo == 0:
                return jnp.concatenate([prev_ref[0, 0, ATTN_TILE - ATTN_BLOCK:, :],
                                        cur_ref[0, 0, :ATTN_BLOCK, :]], axis=0)
            return cur_ref[0, 0, lo - ATTN_BLOCK:lo + ATTN_BLOCK, :]

        def load_qk(lo=lo, with_prev=with_prev):
            return q_ref[0, 0, lo:lo + ATTN_BLOCK, :], with_prev(kc_ref, kp_ref)

        def load_v(with_prev=with_prev):
            return with_prev(vc_ref, vp_ref)

        def consume(m_b, l_b, acc_b, lo=lo):
            _, l_new, acc_new = merged(pl.ds(lo, ATTN_BLOCK), m_b, l_b, acc_b)
            o_ref[0, lo:lo + ATTN_BLOCK, :] = (acc_new / l_new).astype(o_ref.dtype)

        blocks.append((load_qk, 1 if nb == 0 else 0, load_v, consume))

    for load_qk, which_bias, load_v, consume in blocks:
        row_max = score_stage(0, *load_qk(), which_bias)
        consume(*value_stage(0, row_max, load_v()))


def _attn_call(q, k, v):
    B, _, S, _ = q.shape
    cur = pl.BlockSpec((1, 1, ATTN_TILE, LANES), lambda b, s, i: (b, s, i, 0))
    prev = pl.BlockSpec((1, 1, ATTN_TILE, LANES), lambda b, s, i: (b, s, jnp.maximum(i - 1, 0), 0))
    n4 = ATTN_TILE // MID_DILATION
    return pl.pallas_call(
        _attn_kernel,
        grid=(B, N_SLABS, S // ATTN_TILE),
        in_specs=[cur, cur, prev, cur, prev],
        out_specs=pl.BlockSpec((1, ATTN_TILE, LANES), lambda b, s, i: (b, i, s)),
        out_shape=jax.ShapeDtypeStruct((B, S, ATTN_WIDTH), BF16),
        scratch_shapes=[pltpu.VMEM((2, ATTN_BLOCK, 2 * ATTN_BLOCK), F32),
                        pltpu.VMEM((MID_DILATION, n4, LANES), F32),
                        pltpu.VMEM((MID_DILATION, 2 * n4, LANES), F32),
                        pltpu.VMEM((MID_DILATION, 2 * n4, LANES), F32),
                        pltpu.VMEM((ATTN_TILE, LANES), F32),
                        pltpu.VMEM((ATTN_TILE, LANES), F32),
                        pltpu.VMEM((ATTN_TILE, LANES), F32),
                        pltpu.VMEM((1, 2 * ATTN_BLOCK, 2 * ATTN_BLOCK), F32),
                        pltpu.VMEM((2 * ATTN_BLOCK, 2 * ATTN_BLOCK), BF16)],
        compiler_params=pltpu.CompilerParams(
            dimension_semantics=("parallel", "parallel", "parallel"), vmem_limit_bytes=VMEM_LIMIT),
        name="dilated_attn",
    )(q, k, k, v, v)


def _hgrn_scan_matrices():
    C = HGRN_CHUNK
    t = np.arange(C)[:, None]
    u = np.arange(C)[None, :]
    mats = []
    for lvl in range(HGRN_LEVELS):
        start = (t >> lvl) << lvl
        end = start + (1 << lvl) - 1
        is_query = ((t >> lvl) & 1) == 1
        mats.append(np.where(is_query, (u >= start) & (u <= t), (u > t) & (u <= end)))
    mats.append(np.broadcast_to(u <= t, (C, C)))
    mats.append(np.broadcast_to(u > t, (C, C)))
    return np.concatenate(mats, axis=0).astype(np.float32)


def _hgrn_kernel(hq_ref, hf_ref, hi_ref, hg_ref, loglb_ref, log1mlb_ref, onemlb_ref, nw_ref,
                 scan_ref, o_ref, state_ref):
    C = HGRN_CHUNK

    @pl.when(pl.program_id(2) == 0)
    def _():
        state_ref[...] = jnp.zeros_like(state_ref)

    t_idx = lax.broadcasted_iota(jnp.int32, (C, C), 0)
    s_idx = lax.broadcasted_iota(jnp.int32, (C, C), 1)
    differ = t_idx ^ s_idx
    level = jnp.full((C, C), -1, jnp.int32)
    for lvl in range(HGRN_LEVELS):
        level = level + ((differ >> lvl) > 0).astype(jnp.int32)
    level = jnp.where(t_idx < s_idx, -2, level)

    def nt_dot(a, b):
        return lax.dot_general(a, b, NT_DIMS, preferred_element_type=F32)

    for c, head in ((c, h) for c in range(hq_ref.shape[2] // C) for h in range(hq_ref.shape[1])):
        rows = slice(c * C, (c + 1) * C)
        log_lb, log1m_lb, one_m_lb = loglb_ref[head], log1mlb_ref[head], onemlb_ref[head]
        fp = hf_ref[0, head, rows, :]
        hq = hq_ref[0, head, rows, :].astype(F32)
        hg = hg_ref[0, head, rows, :].astype(F32)
        v = hi_ref[0, head, rows, :]

        log_sig = jnp.minimum(fp, 0.0) - jnp.log(1.0 + jnp.exp(-jnp.abs(fp)))
        l2 = log1m_lb + log_sig
        a = jnp.maximum(log_lb, l2) + jnp.log(1.0 + jnp.exp(-jnp.abs(log_lb - l2)))
        k = one_m_lb * jnp.exp(log_sig - fp)
        q = hq * _sigmoid(hq)

        a_hi = a.astype(BF16)
        a_lo = (a - a_hi.astype(F32)).astype(BF16)
        seg = jnp.dot(scan_ref[...], jnp.concatenate([a_hi, a_lo], axis=1),
                      preferred_element_type=F32)
        seg = seg[:, :LANES] + seg[:, LANES:]

        q_bf, k_bf = q.astype(BF16), k.astype(BF16)
        attn = jnp.where(level == -1, nt_dot(q_bf, k_bf), 0.0)
        for lvl in range(HGRN_LEVELS):
            factor = jnp.exp(seg[lvl * C:(lvl + 1) * C]).astype(BF16)
            attn = jnp.where(level == lvl, nt_dot(q_bf * factor, k_bf * factor), attn)

        state = state_ref[head]
        from_start = seg[HGRN_LEVELS * C:(HGRN_LEVELS + 1) * C]
        to_end = seg[(HGRN_LEVELS + 1) * C:(HGRN_LEVELS + 2) * C]
        o = (jnp.dot(attn.astype(BF16), v, preferred_element_type=F32)
             + nt_dot(q_bf * jnp.exp(from_start).astype(BF16), state.astype(BF16)))
        k_out = k_bf * jnp.exp(to_end).astype(BF16)
        chunk_decay = jnp.exp(from_start[C - 1:C, :])
        state_ref[head] = state * chunk_decay + lax.dot_general(
            v, k_out, TN_DIMS, preferred_element_type=F32)

        o = o * lax.rsqrt(jnp.mean(o * o, axis=-1, keepdims=True) + RMS_EPS)
        o = o * nw_ref[head] * (hg * _sigmoid(hg))
        o_ref[0, rows, head * LANES:(head + 1) * LANES] = o.astype(o_ref.dtype)


def _hgrn_call(hq, hf, hi, hg, log_lb, log1m_lb, one_m_lb, norm_w):
    B, H, S, _ = hq.shape
    rows = min(S, HGRN_TILE)
    hps = HGRN_HEADS_PER_STEP
    tile = pl.BlockSpec((1, hps, rows, LANES), lambda b, h, i: (b, h, i, 0))
    per_head = pl.BlockSpec((hps, 1, LANES), lambda b, h, i: (h, 0, 0))
    scan = jnp.asarray(_hgrn_scan_matrices(), dtype=BF16)
    return pl.pallas_call(
        _hgrn_kernel,
        grid=(B, H // hps, S // rows),
        in_specs=[tile, tile, tile, tile, per_head, per_head, per_head, per_head,
                  _resident(scan.shape)],
        out_specs=pl.BlockSpec((1, rows, hps * LANES), lambda b, h, i: (b, i, h)),
        out_shape=jax.ShapeDtypeStruct((B, S, HGRN_WIDTH), BF16),
        scratch_shapes=[pltpu.VMEM((hps, LANES, LANES), F32)],
        compiler_params=pltpu.CompilerParams(
            dimension_semantics=("parallel", "parallel", "arbitrary"), vmem_limit_bytes=VMEM_LIMIT),
        name="hgrn2",
    )(hq, hf, hi, hg, log_lb, log1m_lb, one_m_lb, norm_w, scan)


def kernel(x, c, positions, ln_g, ln_b, ada_w, ada_b, ffn1_w_in, ffn1_w_out, ffn2_w_in, ffn2_w_out,
           mix_w_in, mix_w_out, hgrn_norm_w, hgrn_lb_logits):
    B, S, D = x.shape
    assert D == D_MODEL and S % ATTN_TILE == 0 and S % HGRN_TILE == 0 and S % (FFN_GROUPS * ROW_TILE) == 0

    lb_all = jnp.cumsum(jax.nn.softmax(hgrn_lb_logits.astype(F32), axis=0), axis=0)
    lb_all = (lb_all - lb_all[0:1]).reshape(DEPTH, N_SLABS, 1, LANES)
    log_lb, log1m_lb, one_m_lb = jnp.log(lb_all), jnp.log1p(-lb_all), 1.0 - lb_all
    norm_w = hgrn_norm_w.astype(F32).reshape(DEPTH, N_SLABS, 1, LANES)

    ada = _ada_call(c, ada_w, ada_b).reshape(DEPTH, B, N_SUBLAYERS, 3, 1, D)
    cos, sin_up, sin_dn = _rope_call(positions)

    for l in range(DEPTH):
        def mod(sub, kind, l=l):
            return ada[l, :, sub, kind]

        x = _ffn_call(x, mod(0, 0), mod(0, 1), mod(0, 2), ffn1_w_in[l].astype(BF16),
                      ffn1_w_out[l].astype(BF16), ln_g[l, 0], ln_b[l, 0])

        q, k, v, hq, hf, hi, hg = _proj_call(x, mod(1, 0), mod(1, 1), mix_w_in[l].astype(BF16),
                                             cos, sin_up, sin_dn)
        ao = _attn_call(q, k, v)
        go = _hgrn_call(hq, hf, hi, hg, log_lb[l], log1m_lb[l], one_m_lb[l], norm_w[l])
        x = _mix_out_ffn_call(x, ao, go, mod(1, 2), mix_w_out[l].astype(BF16), ln_g[l, 1], ln_b[l, 1],
                              mod(2, 0), mod(2, 1), mod(2, 2), ffn2_w_in[l].astype(BF16),
                              ffn2_w_out[l].astype(BF16), ln_g[l, 2], ln_b[l, 2])
    return x
```

```python
import functools
import math

import numpy as np
import jax
import jax.numpy as jnp
from jax import lax
from jax.experimental import pallas as pl
from jax.experimental.pallas import tpu as pltpu

D_MODEL = 1024
DEPTH = 2
N_SUBLAYERS = 3
ATTN_WIDTH = 512
HGRN_WIDTH = 512
HEAD_DIM = 64
LANES = 128
N_SLABS = ATTN_WIDTH // LANES
ROPE_THETA = 500000.0
ROPE_DIM = HEAD_DIM // 4
ROPE_HALF = ROPE_DIM // 2
D_FF = 2816
FFN_RES_WEIGHT = 0.5
DEEPNORM_ALPHA = (2 * DEPTH) ** 0.25
LN_EPS = 1e-5
RMS_EPS = 1e-6
IN_COLS = 3 * ATTN_WIDTH + 4 * HGRN_WIDTH
BAND = 128
ATTN_BLOCK = 128
MID_DILATION = 4
MAX_DILATION = 16
ATTN_TILE = ATTN_BLOCK * MAX_DILATION
ATTN_ROW_CHUNK = 32
HGRN_CHUNK = 128
HGRN_LEVELS = 7
HGRN_SMALL_LEVELS = 3
HGRN_TILE = 512
ROW_TILE = 512
FFN_GROUP_ROWS = 256
FFN_GROUPS = 4
HGRN_HEADS_PER_STEP = 4
FF_CHUNK = 256
ADA_COLS = N_SUBLAYERS * 3 * D_MODEL
NEG_BIG = -1e30
LOG2_E = math.log2(math.e)
VMEM_LIMIT = 56 * 1024 * 1024

F32 = jnp.float32
BF16 = jnp.bfloat16
NT_DIMS = (((1,), (1,)), ((), ()))
TN_DIMS = (((0,), (0,)), ((), ()))


def _sigmoid(x):
    return 1.0 / (1.0 + jnp.exp(-x))


def _layer_norm(z, g, b):
    mu = jnp.mean(z, axis=-1, keepdims=True)
    zc = z - mu
    var = jnp.mean(zc * zc, axis=-1, keepdims=True)
    return zc * lax.rsqrt(var + LN_EPS) * g + b


def _resident(shape):
    zeros = (0,) * len(shape)
    return pl.BlockSpec(shape, lambda *_: zeros, pipeline_mode=pl.Buffered(1))


def _ada_kernel(c_ref, w_ref, b_ref, o_ref):
    c = c_ref[...]
    cond = c * _sigmoid(c)
    o_ref[0] = jnp.dot(cond, w_ref[0], preferred_element_type=F32,
                       precision=lax.Precision.HIGHEST) + b_ref[0]


def _ada_call(c, ada_w, ada_b):
    B = c.shape[0]
    ncol = ADA_COLS // D_MODEL
    return pl.pallas_call(
        _ada_kernel,
        grid=(DEPTH, ncol),
        in_specs=[
            pl.BlockSpec((B, D_MODEL), lambda l, j: (0, 0)),
            pl.BlockSpec((1, D_MODEL, D_MODEL), lambda l, j: (l, 0, j)),
            pl.BlockSpec((1, 1, D_MODEL), lambda l, j: (l, 0, j)),
        ],
        out_specs=pl.BlockSpec((1, B, D_MODEL), lambda l, j: (l, 0, j)),
        out_shape=jax.ShapeDtypeStruct((DEPTH, B, ADA_COLS), F32),
        compiler_params=pltpu.CompilerParams(
            dimension_semantics=("parallel", "parallel"), vmem_limit_bytes=VMEM_LIMIT),
        name="ada",
    )(c, ada_w, ada_b.reshape(DEPTH, 1, ADA_COLS))


def _rope_kernel(pos_ref, cos_ref, sin_up_ref, sin_dn_ref):
    pos = pos_ref[0].astype(F32)
    lane = lax.broadcasted_iota(jnp.int32, (1, LANES), 1)
    in_head = lane & (HEAD_DIM - 1)
    freq = (in_head & (ROPE_HALF - 1)).astype(F32)
    inv_freq = jnp.exp(freq * (-math.log(ROPE_THETA) * 2.0 / ROPE_DIM))
    ang = pos * inv_freq
    cos = jnp.cos(ang)
    sin = jnp.sin(ang)
    first = in_head < ROPE_HALF
    second = jnp.logical_and(in_head >= ROPE_HALF, in_head < ROPE_DIM)
    cos_ref[0] = jnp.where(in_head < ROPE_DIM, cos, 1.0)
    sin_up_ref[0] = jnp.where(first, -sin, 0.0)
    sin_dn_ref[0] = jnp.where(second, sin, 0.0)


def _rope_call(positions):
    B, S = positions.shape
    rows = min(S, 1024)
    spec = pl.BlockSpec((1, rows, LANES), lambda b, i: (b, i, 0))
    shape = jax.ShapeDtypeStruct((B, S, LANES), F32)
    return pl.pallas_call(
        _rope_kernel,
        grid=(B, S // rows),
        in_specs=[pl.BlockSpec((1, rows, 1), lambda b, i: (b, i, 0))],
        out_specs=[spec, spec, spec],
        out_shape=[shape, shape, shape],
        compiler_params=pltpu.CompilerParams(
            dimension_semantics=("parallel", "parallel"), vmem_limit_bytes=VMEM_LIMIT),
        name="rope_tables",
    )(positions.reshape(B, S, 1))


def _ffn_rows(x, rows, shift_ref, scale_ref, gate_ref, win_ref, wout_ref, g_ref, b_ref, act_ref):
    h = (x * (1.0 + scale_ref[0]) + shift_ref[0]).astype(BF16)
    for j in range(D_FF // FF_CHUNK):
        lo = j * FF_CHUNK
        g = jnp.dot(h, win_ref[:, lo:lo + FF_CHUNK], preferred_element_type=F32)
        u = jnp.dot(h, win_ref[:, D_FF + lo:D_FF + lo + FF_CHUNK], preferred_element_type=F32)
        act_ref[rows, lo:lo + FF_CHUNK] = (g * _sigmoid(g) * u).astype(BF16)
    y = jnp.dot(act_ref[rows, :], wout_ref[...], preferred_element_type=F32)
    z = DEEPNORM_ALPHA * x + (FFN_RES_WEIGHT * (1.0 + gate_ref[0])) * y
    return _layer_norm(z, g_ref[...], b_ref[...])


def _row_groups(ref):
    return [slice(r * FFN_GROUP_ROWS, (r + 1) * FFN_GROUP_ROWS)
            for r in range(ref.shape[1] // FFN_GROUP_ROWS)]


def _ffn_kernel(x_ref, shift_ref, scale_ref, gate_ref, win_ref, wout_ref, g_ref, b_ref,
                o_ref, act_ref):
    for rows in _row_groups(x_ref):
        o_ref[0, rows, :] = _ffn_rows(x_ref[0, rows, :], rows, shift_ref, scale_ref, gate_ref,
                                      win_ref, wout_ref, g_ref, b_ref, act_ref)


def _mix_out_ffn_kernel(x_ref, ao_ref, go_ref, mgate_ref, wmix_ref, mg_ref, mb_ref,
                        shift_ref, scale_ref, gate_ref, win_ref, wout_ref, g_ref, b_ref,
                        o_ref, act_ref):
    for rows in _row_groups(x_ref):
        y = (jnp.dot(ao_ref[0, rows, :], wmix_ref[:ATTN_WIDTH, :], preferred_element_type=F32)
             + jnp.dot(go_ref[0, rows, :], wmix_ref[ATTN_WIDTH:, :], preferred_element_type=F32))
        z = DEEPNORM_ALPHA * x_ref[0, rows, :] + (1.0 + mgate_ref[0]) * y
        x_mid = _layer_norm(z, mg_ref[...], mb_ref[...])
        o_ref[0, rows, :] = _ffn_rows(x_mid, rows, shift_ref, scale_ref, gate_ref,
                                      win_ref, wout_ref, g_ref, b_ref, act_ref)


def _ffn_specs(D):
    per_batch = pl.BlockSpec((1, 1, D), lambda b, i: (b, 0, 0))
    return [per_batch, per_batch, per_batch, _resident((D, 2 * D_FF)), _resident((D_FF, D)),
            _resident((1, D)), _resident((1, D))]


def _ffn_call(x, shift, scale, gate, w_in, w_out, ln_g, ln_b):
    B, S, D = x.shape
    rows = FFN_GROUPS * FFN_GROUP_ROWS
    tile = pl.BlockSpec((1, rows, D), lambda b, i: (b, i, 0))
    return pl.pallas_call(
        _ffn_kernel,
        grid=(B, S // rows),
        in_specs=[tile] + _ffn_specs(D),
        out_specs=tile,
        out_shape=jax.ShapeDtypeStruct((B, S, D), F32),
        scratch_shapes=[pltpu.VMEM((rows, D_FF), BF16)],
        compiler_params=pltpu.CompilerParams(
            dimension_semantics=("parallel", "parallel"), vmem_limit_bytes=VMEM_LIMIT),
        name="ffn",
    )(x, shift, scale, gate, w_in, w_out, ln_g.reshape(1, D), ln_b.reshape(1, D))


def _mix_out_ffn_call(x, ao, go, mix_gate, w_mix, mix_ln_g, mix_ln_b,
                      shift, scale, gate, w_in, w_out, ln_g, ln_b):
    B, S, D = x.shape
    rows = FFN_GROUPS * FFN_GROUP_ROWS
    tile = pl.BlockSpec((1, rows, D), lambda b, i: (b, i, 0))
    half = pl.BlockSpec((1, rows, ATTN_WIDTH), lambda b, i: (b, i, 0))
    per_batch = pl.BlockSpec((1, 1, D), lambda b, i: (b, 0, 0))
    return pl.pallas_call(
        _mix_out_ffn_kernel,
        grid=(B, S // rows),
        in_specs=[tile, half, half, per_batch, _resident((D, D)), _resident((1, D)), _resident((1, D))]
                 + _ffn_specs(D),
        out_specs=tile,
        out_shape=jax.ShapeDtypeStruct((B, S, D), F32),
        scratch_shapes=[pltpu.VMEM((rows, D_FF), BF16)],
        compiler_params=pltpu.CompilerParams(
            dimension_semantics=("parallel", "parallel"), vmem_limit_bytes=VMEM_LIMIT),
        name="mix_out_ffn",
    )(x, ao, go, mix_gate, w_mix, mix_ln_g.reshape(1, D), mix_ln_b.reshape(1, D),
      shift, scale, gate, w_in, w_out, ln_g.reshape(1, D), ln_b.reshape(1, D))


def _proj_kernel(x_ref, shift_ref, scale_ref, w_ref, cos_ref, sup_ref, sdn_ref,
                 q_ref, k_ref, v_ref, hq_ref, hf_ref, hi_ref, hg_ref):
    x = x_ref[0]
    h = (x * (1.0 + scale_ref[0]) + shift_ref[0]).astype(BF16)
    cos, sup, sdn = cos_ref[0], sup_ref[0], sdn_ref[0]
    outs = (q_ref, k_ref, v_ref, hq_ref, hf_ref, hi_ref, hg_ref)
    for part, out in enumerate(outs):
        p = jnp.dot(h, w_ref[:, part * ATTN_WIDTH:(part + 1) * ATTN_WIDTH],
                    preferred_element_type=F32)
        for s in range(N_SLABS):
            slab = p[:, s * LANES:(s + 1) * LANES]
            if part < 2:
                slab = (slab * cos + pltpu.roll(slab, LANES - ROPE_HALF, 1) * sup
                        + pltpu.roll(slab, ROPE_HALF, 1) * sdn)
                if part == 0:
                    slab = slab * (HEAD_DIM ** -0.5 * LOG2_E)
            out[0, s] = slab.astype(out.dtype)


def _proj_call(x, shift, scale, w, cos, sup, sdn):
    B, S, D = x.shape
    rows = min(S, ROW_TILE)
    tile = pl.BlockSpec((1, rows, D), lambda b, i: (b, i, 0))
    per_batch = pl.BlockSpec((1, 1, D), lambda b, i: (b, 0, 0))
    table = pl.BlockSpec((1, rows, LANES), lambda b, i: (b, i, 0))
    out_spec = pl.BlockSpec((1, N_SLABS, rows, LANES), lambda b, i: (b, 0, i, 0))
    dtypes = (F32, F32, F32, BF16, F32, BF16, BF16)
    return pl.pallas_call(
        _proj_kernel,
        grid=(B, S // rows),
        in_specs=[tile, per_batch, per_batch, _resident((D, IN_COLS)), table, table, table],
        out_specs=[out_spec] * 7,
        out_shape=[jax.ShapeDtypeStruct((B, N_SLABS, S, LANES), dt) for dt in dtypes],
        compiler_params=pltpu.CompilerParams(
            dimension_semantics=("parallel", "parallel"), vmem_limit_bytes=VMEM_LIMIT),
        name="mix_proj",
    )(x, shift, scale, w, cos, sup, sdn)


def _attn_kernel(q_ref, kc_ref, kp_ref, vc_ref, vp_ref, o_ref,
                 bias_ref, q4_ref, k4_ref, v4_ref, m_ref, l_ref, acc_ref, s_ref, p_ref):
    first_tile = pl.program_id(2) == 0
    lane = lax.broadcasted_iota(jnp.int32, (1, LANES), 1)
    head0 = lane < HEAD_DIM
    n4 = ATTN_TILE // MID_DILATION

    row = lax.broadcasted_iota(jnp.int32, (ATTN_BLOCK, 2 * ATTN_BLOCK), 0)
    col = lax.broadcasted_iota(jnp.int32, (ATTN_BLOCK, 2 * ATTN_BLOCK), 1)
    visible = jnp.logical_and(col >= row, col <= row + BAND)
    bias_ref[0] = jnp.where(visible, 0.0, NEG_BIG)
    no_prev = jnp.logical_and(visible, col >= jnp.where(first_tile, ATTN_BLOCK, 0))
    bias_ref[1] = jnp.where(no_prev, 0.0, NEG_BIG)

    for r in range(MID_DILATION):
        q4_ref[r] = q_ref[0, 0, pl.ds(r, n4, stride=MID_DILATION), :]
        k4_ref[r, :n4] = kp_ref[0, 0, pl.ds(r, n4, stride=MID_DILATION), :]
        k4_ref[r, n4:] = kc_ref[0, 0, pl.ds(r, n4, stride=MID_DILATION), :]
        v4_ref[r, :n4] = vp_ref[0, 0, pl.ds(r, n4, stride=MID_DILATION), :]
        v4_ref[r, n4:] = vc_ref[0, 0, pl.ds(r, n4, stride=MID_DILATION), :]

    n_chunks = ATTN_BLOCK // ATTN_ROW_CHUNK

    def score_stage(slot, q, k2, which_bias):
        zero = jnp.zeros_like(q)
        qq = jnp.concatenate([jnp.where(head0, q, zero), jnp.where(head0, zero, q)], axis=0)
        scores = lax.dot_general(qq.astype(BF16), k2.astype(BF16), NT_DIMS, preferred_element_type=F32)
        row_max = [[None] * n_chunks, [None] * n_chunks]
        for head in range(2):
            for c in range(n_chunks):
                lo = c * ATTN_ROW_CHUNK
                rows = slice(head * ATTN_BLOCK + lo, head * ATTN_BLOCK + lo + ATTN_ROW_CHUNK)
                s = scores[rows, :] + bias_ref[which_bias, lo:lo + ATTN_ROW_CHUNK, :]
                s_ref[slot, rows, :] = s
                row_max[head][c] = jnp.max(s, axis=-1, keepdims=True)
        return row_max

    def value_stage(slot, row_max, v2):
        for head in range(2):
            for c in range(n_chunks):
                lo = head * ATTN_BLOCK + c * ATTN_ROW_CHUNK
                p_ref[lo:lo + ATTN_ROW_CHUNK, :] = jnp.exp2(
                    s_ref[slot, lo:lo + ATTN_ROW_CHUNK, :] - row_max[head][c]).astype(BF16)
        v_ones = jnp.concatenate([v2.astype(BF16), jnp.ones((2 * ATTN_BLOCK, LANES), BF16)], axis=1)
        pv = jnp.dot(p_ref[...], v_ones, preferred_element_type=F32)
        acc_b = jnp.where(head0, pv[:ATTN_BLOCK, :LANES], pv[ATTN_BLOCK:, :LANES])
        l_b = jnp.where(head0, pv[:ATTN_BLOCK, LANES:], pv[ATTN_BLOCK:, LANES:])
        m_b = jnp.concatenate([jnp.where(head0, row_max[0][c], row_max[1][c]) for c in range(n_chunks)],
                              axis=0)
        return m_b, l_b, acc_b

    def merged(rows, m_b, l_b, acc_b):
        m_old = m_ref[rows, :]
        m_new = jnp.maximum(m_old, m_b)
        w_old = jnp.exp2(m_old - m_new)
        w_blk = jnp.exp2(m_b - m_new)
        return (m_new, w_old * l_ref[rows, :] + w_blk * l_b, w_old * acc_ref[rows, :] + w_blk * acc_b)

    blocks = []

    for r16 in range(MAX_DILATION):
        r4, j = r16 % MID_DILATION, r16 // MID_DILATION

        def load_qk(r4=r4, j=j):
            return (q4_ref[r4, pl.ds(j, ATTN_BLOCK, stride=MID_DILATION), :],
                    k4_ref[r4, pl.ds(j, 2 * ATTN_BLOCK, stride=MID_DILATION), :])

        def load_v(r4=r4, j=j):
            return v4_ref[r4, pl.ds(j, 2 * ATTN_BLOCK, stride=MID_DILATION), :]

        def consume(m_b, l_b, acc_b, r16=r16):
            rows = pl.ds(r16, ATTN_BLOCK, stride=MAX_DILATION)
            m_ref[rows, :] = m_b
            l_ref[rows, :] = l_b
            acc_ref[rows, :] = acc_b

        blocks.append((load_qk, 1, load_v, consume))

    for r in range(MID_DILATION):
        for nb in range(n4 // ATTN_BLOCK):
            lo = nb * ATTN_BLOCK
            keys = slice(n4 - ATTN_BLOCK + lo, n4 + ATTN_BLOCK + lo)

            def load_qk(r=r, lo=lo, keys=keys):
                return q4_ref[r, lo:lo + ATTN_BLOCK, :], k4_ref[r, keys, :]

            def load_v(r=r, keys=keys):
                return v4_ref[r, keys, :]

            def consume(m_b, l_b, acc_b, r=r, lo=lo):
                rows = pl.ds(r + MID_DILATION * lo, ATTN_BLOCK, stride=MID_DILATION)
                m_new, l_new, acc_new = merged(rows, m_b, l_b, acc_b)
                m_ref[rows, :] = m_new
                l_ref[rows, :] = l_new
                acc_ref[rows, :] = acc_new

            blocks.append((load_qk, 1 if nb == 0 else 0, load_v, consume))

    for nb in range(ATTN_TILE // ATTN_BLOCK):
        lo = nb * ATTN_BLOCK

        def with_prev(cur_ref, prev_ref, lo=lo):
            if lo == 0:
                return jnp.concatenate([prev_ref[0, 0, ATTN_TILE - ATTN_BLOCK:, :],
                                        cur_ref[0, 0, :ATTN_BLOCK, :]], axis=0)
            return cur_ref[0, 0, lo - ATTN_BLOCK:lo + ATTN_BLOCK, :]

        def load_qk(lo=lo, with_prev=with_prev):
            return q_ref[0, 0, lo:lo + ATTN_BLOCK, :], with_prev(kc_ref, kp_ref)

        def load_v(with_prev=with_prev):
            return with_prev(vc_ref, vp_ref)

        def consume(m_b, l_b, acc_b, lo=lo):
            _, l_new, acc_new = merged(pl.ds(lo, ATTN_BLOCK), m_b, l_b, acc_b)
            o_ref[0, lo:lo + ATTN_BLOCK, :] = (acc_new / l_new).astype(o_ref.dtype)

        blocks.append((load_qk, 1 if nb == 0 else 0, load_v, consume))

    for load_qk, which_bias, load_v, consume in blocks:
        row_max = score_stage(0, *load_qk(), which_bias)
        consume(*value_stage(0, row_max, load_v()))


def _attn_call(q, k, v):
    B, _, S, _ = q.shape
    cur = pl.BlockSpec((1, 1, ATTN_TILE, LANES), lambda b, s, i: (b, s, i, 0))
    prev = pl.BlockSpec((1, 1, ATTN_TILE, LANES), lambda b, s, i: (b, s, jnp.maximum(i - 1, 0), 0))
    n4 = ATTN_TILE // MID_DILATION
    return pl.pallas_call(
        _attn_kernel,
        grid=(B, N_SLABS, S // ATTN_TILE),
        in_specs=[cur, cur, prev, cur, prev],
        out_specs=pl.BlockSpec((1, ATTN_TILE, LANES), lambda b, s, i: (b, i, s)),
        out_shape=jax.ShapeDtypeStruct((B, S, ATTN_WIDTH), BF16),
        scratch_shapes=[pltpu.VMEM((2, ATTN_BLOCK, 2 * ATTN_BLOCK), F32),
                        pltpu.VMEM((MID_DILATION, n4, LANES), F32),
                        pltpu.VMEM((MID_DILATION, 2 * n4, LANES), F32),
                        pltpu.VMEM((MID_DILATION, 2 * n4, LANES), F32),
                        pltpu.VMEM((ATTN_TILE, LANES), F32),
                        pltpu.VMEM((ATTN_TILE, LANES), F32),
                        pltpu.VMEM((ATTN_TILE, LANES), F32),
                        pltpu.VMEM((1, 2 * ATTN_BLOCK, 2 * ATTN_BLOCK), F32),
                        pltpu.VMEM((2 * ATTN_BLOCK, 2 * ATTN_BLOCK), BF16)],
        compiler_params=pltpu.CompilerParams(
            dimension_semantics=("parallel", "parallel", "parallel"), vmem_limit_bytes=VMEM_LIMIT),
        name="dilated_attn",
    )(q, k, k, v, v)


def _hgrn_scan_matrices():
    C = HGRN_CHUNK
    t = np.arange(C)[:, None]
    u = np.arange(C)[None, :]
    mats = [np.broadcast_to(u <= t, (C, C))]
    for lvl in range(HGRN_SMALL_LEVELS):
        start = (t >> lvl) << lvl
        end = start + (1 << lvl) - 1
        is_query = ((t >> lvl) & 1) == 1
        mats.append(np.where(is_query, (u >= start) & (u <= t), (u > t) & (u <= end)))
    return np.concatenate(mats, axis=0).astype(np.float32)


def _hgrn_kernel(hq_ref, hf_ref, hi_ref, hg_ref, loglb_ref, log1mlb_ref, onemlb_ref, nw_ref,
                 scan_ref, o_ref, state_ref):
    C = HGRN_CHUNK

    @pl.when(pl.program_id(2) == 0)
    def _():
        state_ref[...] = jnp.zeros_like(state_ref)

    t_idx = lax.broadcasted_iota(jnp.int32, (C, C), 0)
    s_idx = lax.broadcasted_iota(jnp.int32, (C, C), 1)
    differ = t_idx ^ s_idx
    level = jnp.full((C, C), -1, jnp.int32)
    for lvl in range(HGRN_LEVELS):
        level = level + ((differ >> lvl) > 0).astype(jnp.int32)
    level = jnp.where(t_idx < s_idx, -2, level)

    def nt_dot(a, b):
        return lax.dot_general(a, b, NT_DIMS, preferred_element_type=F32)

    items = [(c, h) for c in range(hq_ref.shape[2] // C) for h in range(hq_ref.shape[1])]

    gates = []
    for c, head in items:
        rows = slice(c * C, (c + 1) * C)
        log_lb, log1m_lb, one_m_lb = loglb_ref[head], log1mlb_ref[head], onemlb_ref[head]
        fp = hf_ref[0, head, rows, :]
        hq = hq_ref[0, head, rows, :].astype(F32)

        t = fp * LOG2_E
        log_sig = jnp.minimum(t, 0.0) - jnp.log2(1.0 + jnp.exp2(-jnp.abs(t)))
        l2 = log1m_lb + log_sig
        a = jnp.maximum(log_lb, l2) + jnp.log2(1.0 + jnp.exp2(-jnp.abs(log_lb - l2)))
        k = one_m_lb * jnp.exp2(log_sig - t)
        q = hq / (1.0 + jnp.exp2(hq * (-LOG2_E)))

        a_hi = a.astype(BF16)
        rest = a - a_hi.astype(F32)
        a_mid = rest.astype(BF16)
        a_lo = (rest - a_mid.astype(F32)).astype(BF16)
        b3 = jnp.dot(scan_ref[:C, :], jnp.concatenate([a_hi, a_mid, a_lo], axis=1),
                     preferred_element_type=F32)
        b = b3[:, :LANES] + b3[:, LANES:2 * LANES] + b3[:, 2 * LANES:]
        small = jnp.dot(scan_ref[C:, :], jnp.concatenate([a_hi, a_mid], axis=1),
                        preferred_element_type=F32)
        small = small[:, :LANES] + small[:, LANES:]
        gates.append((q.astype(BF16), k.astype(BF16), b, small))

    def segment_sums(b, small, lvl):
        if lvl < HGRN_SMALL_LEVELS:
            return small[lvl * C:(lvl + 1) * C]
        w = 1 << lvl
        parts = []
        for pair in range(C // (2 * w)):
            lo = pair * 2 * w
            boundary = b[lo + w - 1:lo + w, :]
            parts += [boundary - b[lo:lo + w, :], b[lo + w:lo + 2 * w, :] - boundary]
        return jnp.concatenate(parts, axis=0)

    partial = []
    for (c, head), (q_bf, k_bf, b, small) in zip(items, gates):
        v = hi_ref[0, head, c * C:(c + 1) * C, :]
        attn = jnp.where(level == -1, nt_dot(q_bf, k_bf), 0.0)
        for lvl in range(HGRN_LEVELS):
            factor = jnp.exp2(segment_sums(b, small, lvl)).astype(BF16)
            attn = jnp.where(level == lvl, nt_dot(q_bf * factor, k_bf * factor), attn)
        o_intra = jnp.dot(attn.astype(BF16), v, preferred_element_type=F32)
        q_in = q_bf * jnp.exp2(b).astype(BF16)
        k_out = k_bf * jnp.exp2(b[C - 1:C, :] - b).astype(BF16)
        update = lax.dot_general(v, k_out, TN_DIMS, preferred_element_type=F32)
        partial.append((o_intra, q_in, update, jnp.exp2(b[C - 1:C, :])))

    start_states = []
    for (c, head), (_, _, update, chunk_decay) in zip(items, partial):
        state = state_ref[head]
        start_states.append(state.astype(BF16))
        state_ref[head] = state * chunk_decay + update

    for (c, head), (o_intra, q_in, _, _), state_bf in zip(items, partial, start_states):
        rows = slice(c * C, (c + 1) * C)
        hg = hg_ref[0, head, rows, :].astype(F32)
        o = o_intra + nt_dot(q_in, state_bf)
        o = o * lax.rsqrt(jnp.mean(o * o, axis=-1, keepdims=True) + RMS_EPS)
        o = o * nw_ref[head] * (hg / (1.0 + jnp.exp2(hg * (-LOG2_E))))
        o_ref[0, rows, head * LANES:(head + 1) * LANES] = o.astype(o_ref.dtype)


def _hgrn_call(hq, hf, hi, hg, log_lb, log1m_lb, one_m_lb, norm_w):
    B, H, S, _ = hq.shape
    rows = min(S, HGRN_TILE)
    hps = HGRN_HEADS_PER_STEP
    tile = pl.BlockSpec((1, hps, rows, LANES), lambda b, h, i: (b, h, i, 0))
    per_head = pl.BlockSpec((hps, 1, LANES), lambda b, h, i: (h, 0, 0))
    scan = jnp.asarray(_hgrn_scan_matrices(), dtype=BF16)
    return pl.pallas_call(
        _hgrn_kernel,
        grid=(B, H // hps, S // rows),
        in_specs=[tile, tile, tile, tile, per_head, per_head, per_head, per_head,
                  _resident(scan.shape)],
        out_specs=pl.BlockSpec((1, rows, hps * LANES), lambda b, h, i: (b, i, h)),
        out_shape=jax.ShapeDtypeStruct((B, S, HGRN_WIDTH), BF16),
        scratch_shapes=[pltpu.VMEM((hps, LANES, LANES), F32)],
        compiler_params=pltpu.CompilerParams(
            dimension_semantics=("parallel", "parallel", "arbitrary"), vmem_limit_bytes=VMEM_LIMIT),
        name="hgrn2",
    )(hq, hf, hi, hg, log_lb, log1m_lb, one_m_lb, norm_w, scan)


def kernel(x, c, positions, ln_g, ln_b, ada_w, ada_b, ffn1_w_in, ffn1_w_out, ffn2_w_in, ffn2_w_out,
           mix_w_in, mix_w_out, hgrn_norm_w, hgrn_lb_logits):
    B, S, D = x.shape
    assert D == D_MODEL and S % ATTN_TILE == 0 and S % HGRN_TILE == 0 and S % (FFN_GROUPS * FFN_GROUP_ROWS) == 0 and S % ROW_TILE == 0

    lb_all = jnp.cumsum(jax.nn.softmax(hgrn_lb_logits.astype(F32), axis=0), axis=0)
    lb_all = (lb_all - lb_all[0:1]).reshape(DEPTH, N_SLABS, 1, LANES)
    log_lb, log1m_lb, one_m_lb = jnp.log(lb_all) * LOG2_E, jnp.log1p(-lb_all) * LOG2_E, 1.0 - lb_all
    norm_w = hgrn_norm_w.astype(F32).reshape(DEPTH, N_SLABS, 1, LANES)

    ada = _ada_call(c, ada_w, ada_b).reshape(DEPTH, B, N_SUBLAYERS, 3, 1, D)
    cos, sin_up, sin_dn = _rope_call(positions)

    for l in range(DEPTH):
        def mod(sub, kind, l=l):
            return ada[l, :, sub, kind]

        x = _ffn_call(x, mod(0, 0), mod(0, 1), mod(0, 2), ffn1_w_in[l].astype(BF16),
                      ffn1_w_out[l].astype(BF16), ln_g[l, 0], ln_b[l, 0])

        q, k, v, hq, hf, hi, hg = _proj_call(x, mod(1, 0), mod(1, 1), mix_w_in[l].astype(BF16),
                                             cos, sin_up, sin_dn)
        ao = _attn_call(q, k, v)
        go = _hgrn_call(hq, hf, hi, hg, log_lb[l], log1m_lb[l], one_m_lb[l], norm_w[l])
        x = _mix_out_ffn_call(x, ao, go, mod(1, 2), mix_w_out[l].astype(BF16), ln_g[l, 1], ln_b[l, 1],
                              mod(2, 0), mod(2, 1), mod(2, 2), ffn2_w_in[l].astype(BF16),
                              ffn2_w_out[l].astype(BF16), ln_g[l, 2], ln_b[l, 2])
    return x
```

```python
import functools
import math

import numpy as np
import jax
import jax.numpy as jnp
from jax import lax
from jax.experimental import pallas as pl
from jax.experimental.pallas import tpu as pltpu

D_MODEL = 1024
DEPTH = 2
N_SUBLAYERS = 3
ATTN_WIDTH = 512
HGRN_WIDTH = 512
HEAD_DIM = 64
LANES = 128
N_SLABS = ATTN_WIDTH // LANES
ROPE_THETA = 500000.0
ROPE_DIM = HEAD_DIM // 4
ROPE_HALF = ROPE_DIM // 2
D_FF = 2816
FFN_RES_WEIGHT = 0.5
DEEPNORM_ALPHA = (2 * DEPTH) ** 0.25
LN_EPS = 1e-5
RMS_EPS = 1e-6
IN_COLS = 3 * ATTN_WIDTH + 4 * HGRN_WIDTH
BAND = 128
ATTN_BLOCK = 128
MID_DILATION = 4
MAX_DILATION = 16
ATTN_TILE = ATTN_BLOCK * MAX_DILATION
ATTN_ROW_CHUNK = 32
HGRN_CHUNK = 128
HGRN_LEVELS = 7
HGRN_SMALL_LEVELS = 3
HGRN_TILE = 512
ROW_TILE = 512
FFN_GROUP_ROWS = 256
FFN_GROUPS = 4
FF_CHUNK = 256
ADA_COLS = N_SUBLAYERS * 3 * D_MODEL
NEG_BIG = -1e30
LOG2_E = math.log2(math.e)
VMEM_LIMIT = 56 * 1024 * 1024

F32 = jnp.float32
BF16 = jnp.bfloat16
NT_DIMS = (((1,), (1,)), ((), ()))
TN_DIMS = (((0,), (0,)), ((), ()))


def _sigmoid(x):
    return 1.0 / (1.0 + jnp.exp(-x))


def _layer_norm(z, g, b):
    mu = jnp.mean(z, axis=-1, keepdims=True)
    zc = z - mu
    var = jnp.mean(zc * zc, axis=-1, keepdims=True)
    return zc * lax.rsqrt(var + LN_EPS) * g + b


def _resident(shape):
    zeros = (0,) * len(shape)
    return pl.BlockSpec(shape, lambda *_: zeros, pipeline_mode=pl.Buffered(1))


def _ada_kernel(c_ref, w_ref, b_ref, o_ref):
    c = c_ref[...]
    cond = c * _sigmoid(c)
    o_ref[0] = jnp.dot(cond, w_ref[0], preferred_element_type=F32,
                       precision=lax.Precision.HIGHEST) + b_ref[0]


def _ada_call(c, ada_w, ada_b):
    B = c.shape[0]
    ncol = ADA_COLS // D_MODEL
    return pl.pallas_call(
        _ada_kernel,
        grid=(DEPTH, ncol),
        in_specs=[
            pl.BlockSpec((B, D_MODEL), lambda l, j: (0, 0)),
            pl.BlockSpec((1, D_MODEL, D_MODEL), lambda l, j: (l, 0, j)),
            pl.BlockSpec((1, 1, D_MODEL), lambda l, j: (l, 0, j)),
        ],
        out_specs=pl.BlockSpec((1, B, D_MODEL), lambda l, j: (l, 0, j)),
        out_shape=jax.ShapeDtypeStruct((DEPTH, B, ADA_COLS), F32),
        compiler_params=pltpu.CompilerParams(
            dimension_semantics=("parallel", "parallel"), vmem_limit_bytes=VMEM_LIMIT),
        name="ada",
    )(c, ada_w, ada_b.reshape(DEPTH, 1, ADA_COLS))


def _rope_kernel(pos_ref, cos_ref, sin_up_ref, sin_dn_ref):
    pos = pos_ref[0].astype(F32)
    lane = lax.broadcasted_iota(jnp.int32, (1, LANES), 1)
    in_head = lane & (HEAD_DIM - 1)
    freq = (in_head & (ROPE_HALF - 1)).astype(F32)
    inv_freq = jnp.exp(freq * (-math.log(ROPE_THETA) * 2.0 / ROPE_DIM))
    ang = pos * inv_freq
    cos = jnp.cos(ang)
    sin = jnp.sin(ang)
    first = in_head < ROPE_HALF
    second = jnp.logical_and(in_head >= ROPE_HALF, in_head < ROPE_DIM)
    cos_ref[0] = jnp.where(in_head < ROPE_DIM, cos, 1.0)
    sin_up_ref[0] = jnp.where(first, -sin, 0.0)
    sin_dn_ref[0] = jnp.where(second, sin, 0.0)


def _rope_call(positions):
    B, S = positions.shape
    rows = min(S, 1024)
    spec = pl.BlockSpec((1, rows, LANES), lambda b, i: (b, i, 0))
    shape = jax.ShapeDtypeStruct((B, S, LANES), F32)
    return pl.pallas_call(
        _rope_kernel,
        grid=(B, S // rows),
        in_specs=[pl.BlockSpec((1, rows, 1), lambda b, i: (b, i, 0))],
        out_specs=[spec, spec, spec],
        out_shape=[shape, shape, shape],
        compiler_params=pltpu.CompilerParams(
            dimension_semantics=("parallel", "parallel"), vmem_limit_bytes=VMEM_LIMIT),
        name="rope_tables",
    )(positions.reshape(B, S, 1))


def _ffn_rows(x, rows, shift_ref, scale_ref, gate_ref, win_ref, wout_ref, g_ref, b_ref, act_ref):
    h = (x * (1.0 + scale_ref[0]) + shift_ref[0]).astype(BF16)
    for j in range(D_FF // FF_CHUNK):
        lo = j * FF_CHUNK
        g = jnp.dot(h, win_ref[:, lo:lo + FF_CHUNK], preferred_element_type=F32)
        u = jnp.dot(h, win_ref[:, D_FF + lo:D_FF + lo + FF_CHUNK], preferred_element_type=F32)
        act_ref[rows, lo:lo + FF_CHUNK] = (g * _sigmoid(g) * u).astype(BF16)
    y = jnp.dot(act_ref[rows, :], wout_ref[...], preferred_element_type=F32)
    z = DEEPNORM_ALPHA * x + (FFN_RES_WEIGHT * (1.0 + gate_ref[0])) * y
    return _layer_norm(z, g_ref[...], b_ref[...])


def _row_groups(ref):
    return [slice(r * FFN_GROUP_ROWS, (r + 1) * FFN_GROUP_ROWS)
            for r in range(ref.shape[1] // FFN_GROUP_ROWS)]


def _ffn_kernel(x_ref, shift_ref, scale_ref, gate_ref, win_ref, wout_ref, g_ref, b_ref,
                o_ref, act_ref):
    for rows in _row_groups(x_ref):
        o_ref[0, rows, :] = _ffn_rows(x_ref[0, rows, :], rows, shift_ref, scale_ref, gate_ref,
                                      win_ref, wout_ref, g_ref, b_ref, act_ref)


def _mix_out_ffn_kernel(x_ref, ao_ref, go_ref, mgate_ref, wmix_ref, mg_ref, mb_ref,
                        shift_ref, scale_ref, gate_ref, win_ref, wout_ref, g_ref, b_ref,
                        o_ref, act_ref):
    for rows in _row_groups(x_ref):
        y = (jnp.dot(ao_ref[0, rows, :], wmix_ref[:ATTN_WIDTH, :], preferred_element_type=F32)
             + jnp.dot(go_ref[0, rows, :], wmix_ref[ATTN_WIDTH:, :], preferred_element_type=F32))
        z = DEEPNORM_ALPHA * x_ref[0, rows, :] + (1.0 + mgate_ref[0]) * y
        o_ref[0, rows, :] = _layer_norm(z, mg_ref[...], mb_ref[...])
    for rows in _row_groups(x_ref):
        o_ref[0, rows, :] = _ffn_rows(o_ref[0, rows, :], rows, shift_ref, scale_ref, gate_ref,
                                      win_ref, wout_ref, g_ref, b_ref, act_ref)


def _ffn_specs(D):
    per_batch = pl.BlockSpec((1, 1, D), lambda b, i: (b, 0, 0))
    return [per_batch, per_batch, per_batch, _resident((D, 2 * D_FF)), _resident((D_FF, D)),
            _resident((1, D)), _resident((1, D))]


def _ffn_call(x, shift, scale, gate, w_in, w_out, ln_g, ln_b):
    B, S, D = x.shape
    rows = FFN_GROUPS * FFN_GROUP_ROWS
    tile = pl.BlockSpec((1, rows, D), lambda b, i: (b, i, 0))
    return pl.pallas_call(
        _ffn_kernel,
        grid=(B, S // rows),
        in_specs=[tile] + _ffn_specs(D),
        out_specs=tile,
        out_shape=jax.ShapeDtypeStruct((B, S, D), F32),
        scratch_shapes=[pltpu.VMEM((rows, D_FF), BF16)],
        compiler_params=pltpu.CompilerParams(
            dimension_semantics=("parallel", "parallel"), vmem_limit_bytes=VMEM_LIMIT),
        name="ffn",
    )(x, shift, scale, gate, w_in, w_out, ln_g.reshape(1, D), ln_b.reshape(1, D))


def _mix_out_ffn_call(x, ao, go, mix_gate, w_mix, mix_ln_g, mix_ln_b,
                      shift, scale, gate, w_in, w_out, ln_g, ln_b):
    B, S, D = x.shape
    rows = FFN_GROUPS * FFN_GROUP_ROWS
    tile = pl.BlockSpec((1, rows, D), lambda b, i: (b, i, 0))
    half = pl.BlockSpec((1, rows, ATTN_WIDTH), lambda b, i: (b, i, 0))
    per_batch = pl.BlockSpec((1, 1, D), lambda b, i: (b, 0, 0))
    return pl.pallas_call(
        _mix_out_ffn_kernel,
        grid=(B, S // rows),
        in_specs=[tile, half, half, per_batch, _resident((D, D)), _resident((1, D)), _resident((1, D))]
                 + _ffn_specs(D),
        out_specs=tile,
        out_shape=jax.ShapeDtypeStruct((B, S, D), F32),
        scratch_shapes=[pltpu.VMEM((rows, D_FF), BF16)],
        compiler_params=pltpu.CompilerParams(
            dimension_semantics=("parallel", "parallel"), vmem_limit_bytes=VMEM_LIMIT),
        name="mix_out_ffn",
    )(x, ao, go, mix_gate, w_mix, mix_ln_g.reshape(1, D), mix_ln_b.reshape(1, D),
      shift, scale, gate, w_in, w_out, ln_g.reshape(1, D), ln_b.reshape(1, D))


def _proj_kernel(x_ref, shift_ref, scale_ref, w_ref, cos_ref, sup_ref, sdn_ref,
                 q_ref, k_ref, v_ref, hq_ref, hf_ref, hi_ref, hg_ref):
    x = x_ref[0]
    h = (x * (1.0 + scale_ref[0]) + shift_ref[0]).astype(BF16)
    cos, sup, sdn = cos_ref[0], sup_ref[0], sdn_ref[0]
    outs = (q_ref, k_ref, v_ref, hq_ref, hf_ref, hi_ref, hg_ref)
    for part, out in enumerate(outs):
        p = jnp.dot(h, w_ref[:, part * ATTN_WIDTH:(part + 1) * ATTN_WIDTH],
                    preferred_element_type=F32)
        for s in range(N_SLABS):
            slab = p[:, s * LANES:(s + 1) * LANES]
            if part < 2:
                slab = (slab * cos + pltpu.roll(slab, LANES - ROPE_HALF, 1) * sup
                        + pltpu.roll(slab, ROPE_HALF, 1) * sdn)
                if part == 0:
                    slab = slab * (HEAD_DIM ** -0.5 * LOG2_E)
            out[0, s] = slab.astype(out.dtype)


def _proj_call(x, shift, scale, w, cos, sup, sdn):
    B, S, D = x.shape
    rows = min(S, ROW_TILE)
    tile = pl.BlockSpec((1, rows, D), lambda b, i: (b, i, 0))
    per_batch = pl.BlockSpec((1, 1, D), lambda b, i: (b, 0, 0))
    table = pl.BlockSpec((1, rows, LANES), lambda b, i: (b, i, 0))
    out_spec = pl.BlockSpec((1, N_SLABS, rows, LANES), lambda b, i: (b, 0, i, 0))
    dtypes = (F32, F32, F32, BF16, F32, BF16, BF16)
    return pl.pallas_call(
        _proj_kernel,
        grid=(B, S // rows),
        in_specs=[tile, per_batch, per_batch, _resident((D, IN_COLS)), table, table, table],
        out_specs=[out_spec] * 7,
        out_shape=[jax.ShapeDtypeStruct((B, N_SLABS, S, LANES), dt) for dt in dtypes],
        compiler_params=pltpu.CompilerParams(
            dimension_semantics=("parallel", "parallel"), vmem_limit_bytes=VMEM_LIMIT),
        name="mix_proj",
    )(x, shift, scale, w, cos, sup, sdn)


def _attn_steps(first_tile, q_ref, kc_ref, kp_ref, vc_ref, vp_ref, o_ref,
                bias_ref, q4_ref, k4_ref, v4_ref, m_ref, l_ref, acc_ref, s_ref, p_ref):
    lane = lax.broadcasted_iota(jnp.int32, (1, LANES), 1)
    head0 = lane < HEAD_DIM
    n4 = ATTN_TILE // MID_DILATION

    row = lax.broadcasted_iota(jnp.int32, (ATTN_BLOCK, 2 * ATTN_BLOCK), 0)
    col = lax.broadcasted_iota(jnp.int32, (ATTN_BLOCK, 2 * ATTN_BLOCK), 1)
    visible = jnp.logical_and(col >= row, col <= row + BAND)
    bias_ref[0] = jnp.where(visible, 0.0, NEG_BIG)
    no_prev = jnp.logical_and(visible, col >= jnp.where(first_tile, ATTN_BLOCK, 0))
    bias_ref[1] = jnp.where(no_prev, 0.0, NEG_BIG)

    for r in range(MID_DILATION):
        q4_ref[r] = q_ref[0, 0, pl.ds(r, n4, stride=MID_DILATION), :]
        k4_ref[r, :n4] = kp_ref[0, 0, pl.ds(r, n4, stride=MID_DILATION), :]
        k4_ref[r, n4:] = kc_ref[0, 0, pl.ds(r, n4, stride=MID_DILATION), :]
        v4_ref[r, :n4] = vp_ref[0, 0, pl.ds(r, n4, stride=MID_DILATION), :]
        v4_ref[r, n4:] = vc_ref[0, 0, pl.ds(r, n4, stride=MID_DILATION), :]

    n_chunks = ATTN_BLOCK // ATTN_ROW_CHUNK

    def score_stage(q, k2, which_bias):
        zero = jnp.zeros_like(q)
        qq = jnp.concatenate([jnp.where(head0, q, zero), jnp.where(head0, zero, q)], axis=0)
        scores = lax.dot_general(qq.astype(BF16), k2.astype(BF16), NT_DIMS, preferred_element_type=F32)
        row_max = [[None] * n_chunks, [None] * n_chunks]
        for head in range(2):
            for c in range(n_chunks):
                lo = c * ATTN_ROW_CHUNK
                rows = slice(head * ATTN_BLOCK + lo, head * ATTN_BLOCK + lo + ATTN_ROW_CHUNK)
                s = scores[rows, :] + bias_ref[which_bias, lo:lo + ATTN_ROW_CHUNK, :]
                s_ref[rows, :] = s
                row_max[head][c] = jnp.max(s, axis=-1, keepdims=True)
        return row_max

    def value_stage(row_max, v2):
        for head in range(2):
            for c in range(n_chunks):
                lo = head * ATTN_BLOCK + c * ATTN_ROW_CHUNK
                p_ref[lo:lo + ATTN_ROW_CHUNK, :] = jnp.exp2(
                    s_ref[lo:lo + ATTN_ROW_CHUNK, :] - row_max[head][c]).astype(BF16)
        v_ones = jnp.concatenate([v2.astype(BF16), jnp.ones((2 * ATTN_BLOCK, LANES), BF16)], axis=1)
        pv = jnp.dot(p_ref[...], v_ones, preferred_element_type=F32)
        acc_b = jnp.where(head0, pv[:ATTN_BLOCK, :LANES], pv[ATTN_BLOCK:, :LANES])
        l_b = jnp.where(head0, pv[:ATTN_BLOCK, LANES:], pv[ATTN_BLOCK:, LANES:])
        m_b = jnp.concatenate([jnp.where(head0, row_max[0][c], row_max[1][c]) for c in range(n_chunks)],
                              axis=0)
        return m_b, l_b, acc_b

    def merged(rows, m_b, l_b, acc_b):
        m_old = m_ref[rows, :]
        m_new = jnp.maximum(m_old, m_b)
        w_old = jnp.exp2(m_old - m_new)
        w_blk = jnp.exp2(m_b - m_new)
        return (m_new, w_old * l_ref[rows, :] + w_blk * l_b, w_old * acc_ref[rows, :] + w_blk * acc_b)

    blocks = []

    for r16 in range(MAX_DILATION):
        r4, j = r16 % MID_DILATION, r16 // MID_DILATION

        def load_qk(r4=r4, j=j):
            return (q4_ref[r4, pl.ds(j, ATTN_BLOCK, stride=MID_DILATION), :],
                    k4_ref[r4, pl.ds(j, 2 * ATTN_BLOCK, stride=MID_DILATION), :])

        def load_v(r4=r4, j=j):
            return v4_ref[r4, pl.ds(j, 2 * ATTN_BLOCK, stride=MID_DILATION), :]

        def consume(m_b, l_b, acc_b, r16=r16):
            rows = pl.ds(r16, ATTN_BLOCK, stride=MAX_DILATION)
            m_ref[rows, :] = m_b
            l_ref[rows, :] = l_b
            acc_ref[rows, :] = acc_b

        blocks.append((load_qk, 1, load_v, consume))

    for r in range(MID_DILATION):
        for nb in range(n4 // ATTN_BLOCK):
            lo = nb * ATTN_BLOCK
            keys = slice(n4 - ATTN_BLOCK + lo, n4 + ATTN_BLOCK + lo)

            def load_qk(r=r, lo=lo, keys=keys):
                return q4_ref[r, lo:lo + ATTN_BLOCK, :], k4_ref[r, keys, :]

            def load_v(r=r, keys=keys):
                return v4_ref[r, keys, :]

            def consume(m_b, l_b, acc_b, r=r, lo=lo):
                rows = pl.ds(r + MID_DILATION * lo, ATTN_BLOCK, stride=MID_DILATION)
                m_new, l_new, acc_new = merged(rows, m_b, l_b, acc_b)
                m_ref[rows, :] = m_new
                l_ref[rows, :] = l_new
                acc_ref[rows, :] = acc_new

            blocks.append((load_qk, 1 if nb == 0 else 0, load_v, consume))

    for nb in range(ATTN_TILE // ATTN_BLOCK):
        lo = nb * ATTN_BLOCK

        def with_prev(cur_ref, prev_ref, lo=lo):
            if lo == 0:
                return jnp.concatenate([prev_ref[0, 0, ATTN_TILE - ATTN_BLOCK:, :],
                                        cur_ref[0, 0, :ATTN_BLOCK, :]], axis=0)
            return cur_ref[0, 0, lo - ATTN_BLOCK:lo + ATTN_BLOCK, :]

        def load_qk(lo=lo, with_prev=with_prev):
            return q_ref[0, 0, lo:lo + ATTN_BLOCK, :], with_prev(kc_ref, kp_ref)

        def load_v(with_prev=with_prev):
            return with_prev(vc_ref, vp_ref)

        def consume(m_b, l_b, acc_b, lo=lo):
            _, l_new, acc_new = merged(pl.ds(lo, ATTN_BLOCK), m_b, l_b, acc_b)
            o_ref[0, lo:lo + ATTN_BLOCK, :] = (acc_new / l_new).astype(o_ref.dtype)

        blocks.append((load_qk, 1 if nb == 0 else 0, load_v, consume))

    def run_block(load_qk, which_bias, load_v, consume):
        row_max = score_stage(*load_qk(), which_bias)
        consume(*value_stage(row_max, load_v()))

    return [functools.partial(run_block, *block) for block in blocks]


def _attn_kernel(*refs):
    for step in _attn_steps(pl.program_id(2) == 0, *refs):
        step()


def _attn_call(q, k, v):
    B, _, S, _ = q.shape
    cur = pl.BlockSpec((1, 1, ATTN_TILE, LANES), lambda b, s, i: (b, s, i, 0))
    prev = pl.BlockSpec((1, 1, ATTN_TILE, LANES), lambda b, s, i: (b, s, jnp.maximum(i - 1, 0), 0))
    n4 = ATTN_TILE // MID_DILATION
    return pl.pallas_call(
        _attn_kernel,
        grid=(B, N_SLABS, S // ATTN_TILE),
        in_specs=[cur, cur, prev, cur, prev],
        out_specs=pl.BlockSpec((1, ATTN_TILE, LANES), lambda b, s, i: (b, i, s)),
        out_shape=jax.ShapeDtypeStruct((B, S, ATTN_WIDTH), BF16),
        scratch_shapes=[pltpu.VMEM((2, ATTN_BLOCK, 2 * ATTN_BLOCK), F32),
                        pltpu.VMEM((MID_DILATION, n4, LANES), F32),
                        pltpu.VMEM((MID_DILATION, 2 * n4, LANES), F32),
                        pltpu.VMEM((MID_DILATION, 2 * n4, LANES), F32),
                        pltpu.VMEM((ATTN_TILE, LANES), F32),
                        pltpu.VMEM((ATTN_TILE, LANES), F32),
                        pltpu.VMEM((ATTN_TILE, LANES), F32),
                        pltpu.VMEM((2 * ATTN_BLOCK, 2 * ATTN_BLOCK), F32),
                        pltpu.VMEM((2 * ATTN_BLOCK, 2 * ATTN_BLOCK), BF16)],
        compiler_params=pltpu.CompilerParams(
            dimension_semantics=("parallel", "parallel", "parallel"), vmem_limit_bytes=VMEM_LIMIT),
        name="dilated_attn",
    )(q, k, k, v, v)


def _hgrn_scan_matrices():
    C = HGRN_CHUNK
    t = np.arange(C)[:, None]
    u = np.arange(C)[None, :]
    mats = [np.broadcast_to(u <= t, (C, C))]
    for lvl in range(HGRN_SMALL_LEVELS):
        start = (t >> lvl) << lvl
        end = start + (1 << lvl) - 1
        is_query = ((t >> lvl) & 1) == 1
        mats.append(np.where(is_query, (u >= start) & (u <= t), (u > t) & (u <= end)))
    return np.concatenate(mats, axis=0).astype(np.float32)


def _hgrn_steps(first_step, hq_ref, hf_ref, hi_ref, hg_ref, loglb_ref, log1mlb_ref, onemlb_ref, nw_ref,
                scan_ref, o_ref, state_ref):
    C = HGRN_CHUNK

    @pl.when(first_step)
    def _():
        state_ref[...] = jnp.zeros_like(state_ref)

    t_idx = lax.broadcasted_iota(jnp.int32, (C, C), 0)
    s_idx = lax.broadcasted_iota(jnp.int32, (C, C), 1)
    differ = t_idx ^ s_idx
    level = jnp.full((C, C), -1, jnp.int32)
    for lvl in range(HGRN_LEVELS):
        level = level + ((differ >> lvl) > 0).astype(jnp.int32)
    level = jnp.where(t_idx < s_idx, -2, level)

    def nt_dot(a, b):
        return lax.dot_general(a, b, NT_DIMS, preferred_element_type=F32)

    items = [(c, h) for c in range(hq_ref.shape[2] // C) for h in range(hq_ref.shape[1])]
    gates, partial, start_states = {}, {}, {}

    def gate_step(c, head):
        rows = slice(c * C, (c + 1) * C)
        log_lb, log1m_lb, one_m_lb = loglb_ref[head], log1mlb_ref[head], onemlb_ref[head]
        fp = hf_ref[0, head, rows, :]
        hq = hq_ref[0, head, rows, :].astype(F32)

        t = fp * LOG2_E
        log_sig = jnp.minimum(t, 0.0) - jnp.log2(1.0 + jnp.exp2(-jnp.abs(t)))
        l2 = log1m_lb + log_sig
        a = jnp.maximum(log_lb, l2) + jnp.log2(1.0 + jnp.exp2(-jnp.abs(log_lb - l2)))
        k = one_m_lb * jnp.exp2(log_sig - t)
        q = hq / (1.0 + jnp.exp2(hq * (-LOG2_E)))

        a_hi = a.astype(BF16)
        rest = a - a_hi.astype(F32)
        a_mid = rest.astype(BF16)
        a_lo = (rest - a_mid.astype(F32)).astype(BF16)
        b3 = jnp.dot(scan_ref[:C, :], jnp.concatenate([a_hi, a_mid, a_lo], axis=1),
                     preferred_element_type=F32)
        b = b3[:, :LANES] + b3[:, LANES:2 * LANES] + b3[:, 2 * LANES:]
        small = jnp.dot(scan_ref[C:, :], jnp.concatenate([a_hi, a_mid], axis=1),
                        preferred_element_type=F32)
        small = small[:, :LANES] + small[:, LANES:]
        gates[c, head] = (q.astype(BF16), k.astype(BF16), b, small)

    def segment_sums(b, small, lvl):
        if lvl < HGRN_SMALL_LEVELS:
            return small[lvl * C:(lvl + 1) * C]
        w = 1 << lvl
        parts = []
        for pair in range(C // (2 * w)):
            lo = pair * 2 * w
            boundary = b[lo + w - 1:lo + w, :]
            parts += [boundary - b[lo:lo + w, :], b[lo + w:lo + 2 * w, :] - boundary]
        return jnp.concatenate(parts, axis=0)

    def score_step(c, head):
        q_bf, k_bf, b, small = gates[c, head]
        v = hi_ref[0, head, c * C:(c + 1) * C, :]
        attn = jnp.where(level == -1, nt_dot(q_bf, k_bf), 0.0)
        for lvl in range(HGRN_LEVELS):
            factor = jnp.exp2(segment_sums(b, small, lvl)).astype(BF16)
            attn = jnp.where(level == lvl, nt_dot(q_bf * factor, k_bf * factor), attn)
        o_intra = jnp.dot(attn.astype(BF16), v, preferred_element_type=F32)
        q_in = q_bf * jnp.exp2(b).astype(BF16)
        k_out = k_bf * jnp.exp2(b[C - 1:C, :] - b).astype(BF16)
        update = lax.dot_general(v, k_out, TN_DIMS, preferred_element_type=F32)
        partial[c, head] = (o_intra, q_in, update, jnp.exp2(b[C - 1:C, :]))

    def state_step():
        for c, head in items:
            _, _, update, chunk_decay = partial[c, head]
            state = state_ref[head]
            start_states[c, head] = state.astype(BF16)
            state_ref[head] = state * chunk_decay + update

    def output_step(c, head):
        o_intra, q_in, _, _ = partial[c, head]
        rows = slice(c * C, (c + 1) * C)
        hg = hg_ref[0, head, rows, :].astype(F32)
        o = o_intra + nt_dot(q_in, start_states[c, head])
        o = o * lax.rsqrt(jnp.mean(o * o, axis=-1, keepdims=True) + RMS_EPS)
        o = o * nw_ref[head] * (hg / (1.0 + jnp.exp2(hg * (-LOG2_E))))
        o_ref[0, rows, head * LANES:(head + 1) * LANES] = o.astype(o_ref.dtype)

    return ([functools.partial(gate_step, *item) for item in items]
            + [functools.partial(score_step, *item) for item in items]
            + [state_step]
            + [functools.partial(output_step, *item) for item in items])


def _hgrn_kernel(*refs):
    for step in _hgrn_steps(pl.program_id(1) == 0, *refs):
        step()


def _hgrn_call(hq, hf, hi, hg, log_lb, log1m_lb, one_m_lb, norm_w):
    B, H, S, _ = hq.shape
    tile = pl.BlockSpec((1, H, HGRN_TILE, LANES), lambda b, i: (b, 0, i, 0))
    per_head = _resident((H, 1, LANES))
    scan = jnp.asarray(_hgrn_scan_matrices(), dtype=BF16)
    return pl.pallas_call(
        _hgrn_kernel,
        grid=(B, S // HGRN_TILE),
        in_specs=[tile, tile, tile, tile, per_head, per_head, per_head, per_head,
                  _resident(scan.shape)],
        out_specs=pl.BlockSpec((1, HGRN_TILE, H * LANES), lambda b, i: (b, i, 0)),
        out_shape=jax.ShapeDtypeStruct((B, S, HGRN_WIDTH), BF16),
        scratch_shapes=[pltpu.VMEM((H, LANES, LANES), F32)],
        compiler_params=pltpu.CompilerParams(
            dimension_semantics=("parallel", "arbitrary"), vmem_limit_bytes=VMEM_LIMIT),
        name="hgrn2",
    )(hq, hf, hi, hg, log_lb, log1m_lb, one_m_lb, norm_w, scan)


def kernel(x, c, positions, ln_g, ln_b, ada_w, ada_b, ffn1_w_in, ffn1_w_out, ffn2_w_in, ffn2_w_out,
           mix_w_in, mix_w_out, hgrn_norm_w, hgrn_lb_logits):
    B, S, D = x.shape
    assert D == D_MODEL and S % ATTN_TILE == 0 and S % HGRN_TILE == 0
    assert S % (FFN_GROUPS * FFN_GROUP_ROWS) == 0 and S % ROW_TILE == 0

    lb_all = jnp.cumsum(jax.nn.softmax(hgrn_lb_logits.astype(F32), axis=0), axis=0)
    lb_all = (lb_all - lb_all[0:1]).reshape(DEPTH, N_SLABS, 1, LANES)
    log_lb, log1m_lb, one_m_lb = jnp.log(lb_all) * LOG2_E, jnp.log1p(-lb_all) * LOG2_E, 1.0 - lb_all
    norm_w = hgrn_norm_w.astype(F32).reshape(DEPTH, N_SLABS, 1, LANES)

    ada = _ada_call(c, ada_w, ada_b).reshape(DEPTH, B, N_SUBLAYERS, 3, 1, D)
    cos, sin_up, sin_dn = _rope_call(positions)

    for l in range(DEPTH):
        def mod(sub, kind, l=l):
            return ada[l, :, sub, kind]

        x = _ffn_call(x, mod(0, 0), mod(0, 1), mod(0, 2), ffn1_w_in[l].astype(BF16),
                      ffn1_w_out[l].astype(BF16), ln_g[l, 0], ln_b[l, 0])

        q, k, v, hq, hf, hi, hg = _proj_call(x, mod(1, 0), mod(1, 1), mix_w_in[l].astype(BF16),
                                             cos, sin_up, sin_dn)
        ao = _attn_call(q, k, v)
        go = _hgrn_call(hq, hf, hi, hg, log_lb[l], log1m_lb[l], one_m_lb[l], norm_w[l])
        x = _mix_out_ffn_call(x, ao, go, mod(1, 2), mix_w_out[l].astype(BF16), ln_g[l, 1], ln_b[l, 1],
                              mod(2, 0), mod(2, 1), mod(2, 2), ffn2_w_in[l].astype(BF16),
                              ffn2_w_out[l].astype(BF16), ln_g[l, 2], ln_b[l, 2])
    return x
```

```python
import functools
import math

import numpy as np
import jax
import jax.numpy as jnp
from jax import lax
from jax.experimental import pallas as pl
from jax.experimental.pallas import tpu as pltpu

D_MODEL = 1024
DEPTH = 2
N_SUBLAYERS = 3
ATTN_WIDTH = 512
HGRN_WIDTH = 512
HEAD_DIM = 64
LANES = 128
N_SLABS = ATTN_WIDTH // LANES
ROPE_THETA = 500000.0
ROPE_DIM = HEAD_DIM // 4
ROPE_HALF = ROPE_DIM // 2
D_FF = 2816
FFN_RES_WEIGHT = 0.5
DEEPNORM_ALPHA = (2 * DEPTH) ** 0.25
LN_EPS = 1e-5
RMS_EPS = 1e-6
IN_COLS = 3 * ATTN_WIDTH + 4 * HGRN_WIDTH
BAND = 128
ATTN_BLOCK = 128
MID_DILATION = 4
MAX_DILATION = 16
ATTN_TILE = ATTN_BLOCK * MAX_DILATION
ATTN_ROW_CHUNK = 32
HGRN_CHUNK = 128
HGRN_LEVELS = 7
HGRN_SMALL_LEVELS = 3
HGRN_TILE = 1024
ATTN_SLABS_PER_STEP = 2
ROW_TILE = 512
FFN_GROUP_ROWS = 256
FFN_GROUPS = 4
FF_CHUNK = 256
ADA_COLS = N_SUBLAYERS * 3 * D_MODEL
NEG_BIG = -1e30
LOG2_E = math.log2(math.e)
VMEM_LIMIT = 56 * 1024 * 1024

F32 = jnp.float32
BF16 = jnp.bfloat16
NT_DIMS = (((1,), (1,)), ((), ()))
TN_DIMS = (((0,), (0,)), ((), ()))


def _sigmoid(x):
    return 1.0 / (1.0 + jnp.exp(-x))


def _layer_norm(z, g, b):
    mu = jnp.mean(z, axis=-1, keepdims=True)
    zc = z - mu
    var = jnp.mean(zc * zc, axis=-1, keepdims=True)
    return zc * lax.rsqrt(var + LN_EPS) * g + b


def _resident(shape):
    zeros = (0,) * len(shape)
    return pl.BlockSpec(shape, lambda *_: zeros, pipeline_mode=pl.Buffered(1))


def _ada_kernel(c_ref, w_ref, b_ref, o_ref):
    c = c_ref[...]
    cond = c * _sigmoid(c)
    o_ref[0] = jnp.dot(cond, w_ref[0], preferred_element_type=F32,
                       precision=lax.Precision.HIGHEST) + b_ref[0]


def _ada_call(c, ada_w, ada_b):
    B = c.shape[0]
    ncol = ADA_COLS // D_MODEL
    return pl.pallas_call(
        _ada_kernel,
        grid=(DEPTH, ncol),
        in_specs=[
            pl.BlockSpec((B, D_MODEL), lambda l, j: (0, 0)),
            pl.BlockSpec((1, D_MODEL, D_MODEL), lambda l, j: (l, 0, j)),
            pl.BlockSpec((1, 1, D_MODEL), lambda l, j: (l, 0, j)),
        ],
        out_specs=pl.BlockSpec((1, B, D_MODEL), lambda l, j: (l, 0, j)),
        out_shape=jax.ShapeDtypeStruct((DEPTH, B, ADA_COLS), F32),
        compiler_params=pltpu.CompilerParams(
            dimension_semantics=("parallel", "parallel"), vmem_limit_bytes=VMEM_LIMIT),
        name="ada",
    )(c, ada_w, ada_b.reshape(DEPTH, 1, ADA_COLS))


def _rope_kernel(pos_ref, cos_ref, sin_up_ref, sin_dn_ref):
    pos = pos_ref[0].astype(F32)
    lane = lax.broadcasted_iota(jnp.int32, (1, LANES), 1)
    in_head = lane & (HEAD_DIM - 1)
    freq = (in_head & (ROPE_HALF - 1)).astype(F32)
    inv_freq = jnp.exp(freq * (-math.log(ROPE_THETA) * 2.0 / ROPE_DIM))
    ang = pos * inv_freq
    cos = jnp.cos(ang)
    sin = jnp.sin(ang)
    first = in_head < ROPE_HALF
    second = jnp.logical_and(in_head >= ROPE_HALF, in_head < ROPE_DIM)
    cos_ref[0] = jnp.where(in_head < ROPE_DIM, cos, 1.0)
    sin_up_ref[0] = jnp.where(first, -sin, 0.0)
    sin_dn_ref[0] = jnp.where(second, sin, 0.0)


def _rope_call(positions):
    B, S = positions.shape
    rows = min(S, 1024)
    spec = pl.BlockSpec((1, rows, LANES), lambda b, i: (b, i, 0))
    shape = jax.ShapeDtypeStruct((B, S, LANES), F32)
    return pl.pallas_call(
        _rope_kernel,
        grid=(B, S // rows),
        in_specs=[pl.BlockSpec((1, rows, 1), lambda b, i: (b, i, 0))],
        out_specs=[spec, spec, spec],
        out_shape=[shape, shape, shape],
        compiler_params=pltpu.CompilerParams(
            dimension_semantics=("parallel", "parallel"), vmem_limit_bytes=VMEM_LIMIT),
        name="rope_tables",
    )(positions.reshape(B, S, 1))


def _ffn_rows(x, rows, shift_ref, scale_ref, gate_ref, win_ref, wout_ref, g_ref, b_ref, act_ref):
    h = (x * (1.0 + scale_ref[0]) + shift_ref[0]).astype(BF16)
    for j in range(D_FF // FF_CHUNK):
        lo = j * FF_CHUNK
        g = jnp.dot(h, win_ref[:, lo:lo + FF_CHUNK], preferred_element_type=F32)
        u = jnp.dot(h, win_ref[:, D_FF + lo:D_FF + lo + FF_CHUNK], preferred_element_type=F32)
        act_ref[rows, lo:lo + FF_CHUNK] = (g * _sigmoid(g) * u).astype(BF16)
    y = jnp.dot(act_ref[rows, :], wout_ref[...], preferred_element_type=F32)
    z = DEEPNORM_ALPHA * x + (FFN_RES_WEIGHT * (1.0 + gate_ref[0])) * y
    return _layer_norm(z, g_ref[...], b_ref[...])


def _row_groups(ref):
    return [slice(r * FFN_GROUP_ROWS, (r + 1) * FFN_GROUP_ROWS)
            for r in range(ref.shape[1] // FFN_GROUP_ROWS)]


def _ffn_kernel(x_ref, shift_ref, scale_ref, gate_ref, win_ref, wout_ref, g_ref, b_ref,
                o_ref, act_ref):
    for rows in _row_groups(x_ref):
        o_ref[0, rows, :] = _ffn_rows(x_ref[0, rows, :], rows, shift_ref, scale_ref, gate_ref,
                                      win_ref, wout_ref, g_ref, b_ref, act_ref)


def _mix_out_ffn_kernel(x_ref, ao_ref, go_ref, mgate_ref, wmix_ref, mg_ref, mb_ref,
                        shift_ref, scale_ref, gate_ref, win_ref, wout_ref, g_ref, b_ref,
                        o_ref, act_ref):
    for rows in _row_groups(x_ref):
        y = (jnp.dot(ao_ref[0, rows, :], wmix_ref[:ATTN_WIDTH, :], preferred_element_type=F32)
             + jnp.dot(go_ref[0, rows, :], wmix_ref[ATTN_WIDTH:, :], preferred_element_type=F32))
        z = DEEPNORM_ALPHA * x_ref[0, rows, :] + (1.0 + mgate_ref[0]) * y
        o_ref[0, rows, :] = _layer_norm(z, mg_ref[...], mb_ref[...])
    for rows in _row_groups(x_ref):
        o_ref[0, rows, :] = _ffn_rows(o_ref[0, rows, :], rows, shift_ref, scale_ref, gate_ref,
                                      win_ref, wout_ref, g_ref, b_ref, act_ref)


def _ffn_specs(D):
    per_batch = pl.BlockSpec((1, 1, D), lambda b, i: (b, 0, 0))
    return [per_batch, per_batch, per_batch, _resident((D, 2 * D_FF)), _resident((D_FF, D)),
            _resident((1, D)), _resident((1, D))]


def _ffn_call(x, shift, scale, gate, w_in, w_out, ln_g, ln_b):
    B, S, D = x.shape
    rows = FFN_GROUPS * FFN_GROUP_ROWS
    tile = pl.BlockSpec((1, rows, D), lambda b, i: (b, i, 0))
    return pl.pallas_call(
        _ffn_kernel,
        grid=(B, S // rows),
        in_specs=[tile] + _ffn_specs(D),
        out_specs=tile,
        out_shape=jax.ShapeDtypeStruct((B, S, D), F32),
        scratch_shapes=[pltpu.VMEM((rows, D_FF), BF16)],
        compiler_params=pltpu.CompilerParams(
            dimension_semantics=("parallel", "parallel"), vmem_limit_bytes=VMEM_LIMIT),
        name="ffn",
    )(x, shift, scale, gate, w_in, w_out, ln_g.reshape(1, D), ln_b.reshape(1, D))


def _mix_out_ffn_call(x, ao, go, mix_gate, w_mix, mix_ln_g, mix_ln_b,
                      shift, scale, gate, w_in, w_out, ln_g, ln_b):
    B, S, D = x.shape
    rows = FFN_GROUPS * FFN_GROUP_ROWS
    tile = pl.BlockSpec((1, rows, D), lambda b, i: (b, i, 0))
    half = pl.BlockSpec((1, rows, ATTN_WIDTH), lambda b, i: (b, i, 0))
    per_batch = pl.BlockSpec((1, 1, D), lambda b, i: (b, 0, 0))
    return pl.pallas_call(
        _mix_out_ffn_kernel,
        grid=(B, S // rows),
        in_specs=[tile, half, half, per_batch, _resident((D, D)), _resident((1, D)), _resident((1, D))]
                 + _ffn_specs(D),
        out_specs=tile,
        out_shape=jax.ShapeDtypeStruct((B, S, D), F32),
        scratch_shapes=[pltpu.VMEM((rows, D_FF), BF16)],
        compiler_params=pltpu.CompilerParams(
            dimension_semantics=("parallel", "parallel"), vmem_limit_bytes=VMEM_LIMIT),
        name="mix_out_ffn",
    )(x, ao, go, mix_gate, w_mix, mix_ln_g.reshape(1, D), mix_ln_b.reshape(1, D),
      shift, scale, gate, w_in, w_out, ln_g.reshape(1, D), ln_b.reshape(1, D))


def _proj_kernel(x_ref, shift_ref, scale_ref, w_ref, cos_ref, sup_ref, sdn_ref,
                 q_ref, k_ref, v_ref, hq_ref, hf_ref, hi_ref, hg_ref):
    x = x_ref[0]
    h = (x * (1.0 + scale_ref[0]) + shift_ref[0]).astype(BF16)
    cos, sup, sdn = cos_ref[0], sup_ref[0], sdn_ref[0]
    outs = (q_ref, k_ref, v_ref, hq_ref, hf_ref, hi_ref, hg_ref)
    for part, out in enumerate(outs):
        p = jnp.dot(h, w_ref[:, part * ATTN_WIDTH:(part + 1) * ATTN_WIDTH],
                    preferred_element_type=F32)
        for s in range(N_SLABS):
            slab = p[:, s * LANES:(s + 1) * LANES]
            if part < 2:
                slab = (slab * cos + pltpu.roll(slab, LANES - ROPE_HALF, 1) * sup
                        + pltpu.roll(slab, ROPE_HALF, 1) * sdn)
                if part == 0:
                    slab = slab * (HEAD_DIM ** -0.5 * LOG2_E)
            out[0, s] = slab.astype(out.dtype)


def _proj_call(x, shift, scale, w, cos, sup, sdn):
    B, S, D = x.shape
    rows = min(S, ROW_TILE)
    tile = pl.BlockSpec((1, rows, D), lambda b, i: (b, i, 0))
    per_batch = pl.BlockSpec((1, 1, D), lambda b, i: (b, 0, 0))
    table = pl.BlockSpec((1, rows, LANES), lambda b, i: (b, i, 0))
    out_spec = pl.BlockSpec((1, N_SLABS, rows, LANES), lambda b, i: (b, 0, i, 0))
    dtypes = (F32, F32, F32, BF16, F32, BF16, BF16)
    return pl.pallas_call(
        _proj_kernel,
        grid=(B, S // rows),
        in_specs=[tile, per_batch, per_batch, _resident((D, IN_COLS)), table, table, table],
        out_specs=[out_spec] * 7,
        out_shape=[jax.ShapeDtypeStruct((B, N_SLABS, S, LANES), dt) for dt in dtypes],
        compiler_params=pltpu.CompilerParams(
            dimension_semantics=("parallel", "parallel"), vmem_limit_bytes=VMEM_LIMIT),
        name="mix_proj",
    )(x, shift, scale, w, cos, sup, sdn)


def _attn_steps(first_tile, slab, q_ref, kc_ref, kp_ref, vc_ref, vp_ref, o_ref,
                bias_ref, q4_ref, k4_ref, v4_ref, m_ref, l_ref, acc_ref, s_ref, p_ref):
    lane = lax.broadcasted_iota(jnp.int32, (1, LANES), 1)
    head0 = lane < HEAD_DIM
    n4 = ATTN_TILE // MID_DILATION

    row = lax.broadcasted_iota(jnp.int32, (ATTN_BLOCK, 2 * ATTN_BLOCK), 0)
    col = lax.broadcasted_iota(jnp.int32, (ATTN_BLOCK, 2 * ATTN_BLOCK), 1)
    visible = jnp.logical_and(col >= row, col <= row + BAND)
    bias_ref[0] = jnp.where(visible, 0.0, NEG_BIG)
    no_prev = jnp.logical_and(visible, col >= jnp.where(first_tile, ATTN_BLOCK, 0))
    bias_ref[1] = jnp.where(no_prev, 0.0, NEG_BIG)

    for r in range(MID_DILATION):
        q4_ref[r] = q_ref[0, slab, pl.ds(r, n4, stride=MID_DILATION), :]
        k4_ref[r, :n4] = kp_ref[0, slab, pl.ds(r, n4, stride=MID_DILATION), :]
        k4_ref[r, n4:] = kc_ref[0, slab, pl.ds(r, n4, stride=MID_DILATION), :]
        v4_ref[r, :n4] = vp_ref[0, slab, pl.ds(r, n4, stride=MID_DILATION), :]
        v4_ref[r, n4:] = vc_ref[0, slab, pl.ds(r, n4, stride=MID_DILATION), :]

    n_chunks = ATTN_BLOCK // ATTN_ROW_CHUNK

    def score_stage(q, k2, which_bias):
        zero = jnp.zeros_like(q)
        qq = jnp.concatenate([jnp.where(head0, q, zero), jnp.where(head0, zero, q)], axis=0)
        scores = lax.dot_general(qq.astype(BF16), k2.astype(BF16), NT_DIMS, preferred_element_type=F32)
        row_max = [[None] * n_chunks, [None] * n_chunks]
        for head in range(2):
            for c in range(n_chunks):
                lo = c * ATTN_ROW_CHUNK
                rows = slice(head * ATTN_BLOCK + lo, head * ATTN_BLOCK + lo + ATTN_ROW_CHUNK)
                s = scores[rows, :] + bias_ref[which_bias, lo:lo + ATTN_ROW_CHUNK, :]
                s_ref[rows, :] = s
                row_max[head][c] = jnp.max(s, axis=-1, keepdims=True)
        return row_max

    def value_stage(row_max, v2):
        for head in range(2):
            for c in range(n_chunks):
                lo = head * ATTN_BLOCK + c * ATTN_ROW_CHUNK
                p_ref[lo:lo + ATTN_ROW_CHUNK, :] = jnp.exp2(
                    s_ref[lo:lo + ATTN_ROW_CHUNK, :] - row_max[head][c]).astype(BF16)
        v_ones = jnp.concatenate([v2.astype(BF16), jnp.ones((2 * ATTN_BLOCK, LANES), BF16)], axis=1)
        pv = jnp.dot(p_ref[...], v_ones, preferred_element_type=F32)
        acc_b = jnp.where(head0, pv[:ATTN_BLOCK, :LANES], pv[ATTN_BLOCK:, :LANES])
        l_b = jnp.where(head0, pv[:ATTN_BLOCK, LANES:], pv[ATTN_BLOCK:, LANES:])
        m_b = jnp.concatenate([jnp.where(head0, row_max[0][c], row_max[1][c]) for c in range(n_chunks)],
                              axis=0)
        return m_b, l_b, acc_b

    def merged(rows, m_b, l_b, acc_b):
        m_old = m_ref[rows, :]
        m_new = jnp.maximum(m_old, m_b)
        w_old = jnp.exp2(m_old - m_new)
        w_blk = jnp.exp2(m_b - m_new)
        return (m_new, w_old * l_ref[rows, :] + w_blk * l_b, w_old * acc_ref[rows, :] + w_blk * acc_b)

    blocks = []

    for r16 in range(MAX_DILATION):
        r4, j = r16 % MID_DILATION, r16 // MID_DILATION

        def load_qk(r4=r4, j=j):
            return (q4_ref[r4, pl.ds(j, ATTN_BLOCK, stride=MID_DILATION), :],
                    k4_ref[r4, pl.ds(j, 2 * ATTN_BLOCK, stride=MID_DILATION), :])

        def load_v(r4=r4, j=j):
            return v4_ref[r4, pl.ds(j, 2 * ATTN_BLOCK, stride=MID_DILATION), :]

        def consume(m_b, l_b, acc_b, r16=r16):
            rows = pl.ds(r16, ATTN_BLOCK, stride=MAX_DILATION)
            m_ref[rows, :] = m_b
            l_ref[rows, :] = l_b
            acc_ref[rows, :] = acc_b

        blocks.append((load_qk, 1, load_v, consume))

    for r in range(MID_DILATION):
        for nb in range(n4 // ATTN_BLOCK):
            lo = nb * ATTN_BLOCK
            keys = slice(n4 - ATTN_BLOCK + lo, n4 + ATTN_BLOCK + lo)

            def load_qk(r=r, lo=lo, keys=keys):
                return q4_ref[r, lo:lo + ATTN_BLOCK, :], k4_ref[r, keys, :]

            def load_v(r=r, keys=keys):
                return v4_ref[r, keys, :]

            def consume(m_b, l_b, acc_b, r=r, lo=lo):
                rows = pl.ds(r + MID_DILATION * lo, ATTN_BLOCK, stride=MID_DILATION)
                m_new, l_new, acc_new = merged(rows, m_b, l_b, acc_b)
                m_ref[rows, :] = m_new
                l_ref[rows, :] = l_new
                acc_ref[rows, :] = acc_new

            blocks.append((load_qk, 1 if nb == 0 else 0, load_v, consume))

    for nb in range(ATTN_TILE // ATTN_BLOCK):
        lo = nb * ATTN_BLOCK

        def with_prev(cur_ref, prev_ref, lo=lo):
            if lo == 0:
                return jnp.concatenate([prev_ref[0, slab, ATTN_TILE - ATTN_BLOCK:, :],
                                        cur_ref[0, slab, :ATTN_BLOCK, :]], axis=0)
            return cur_ref[0, slab, lo - ATTN_BLOCK:lo + ATTN_BLOCK, :]

        def load_qk(lo=lo, with_prev=with_prev):
            return q_ref[0, slab, lo:lo + ATTN_BLOCK, :], with_prev(kc_ref, kp_ref)

        def load_v(with_prev=with_prev):
            return with_prev(vc_ref, vp_ref)

        def consume(m_b, l_b, acc_b, lo=lo):
            _, l_new, acc_new = merged(pl.ds(lo, ATTN_BLOCK), m_b, l_b, acc_b)
            o_ref[0, lo:lo + ATTN_BLOCK, slab * LANES:(slab + 1) * LANES] = (
                acc_new / l_new).astype(o_ref.dtype)

        blocks.append((load_qk, 1 if nb == 0 else 0, load_v, consume))

    def run_block(load_qk, which_bias, load_v, consume):
        row_max = score_stage(*load_qk(), which_bias)
        consume(*value_stage(row_max, load_v()))

    return [functools.partial(run_block, *block) for block in blocks]


def _attn_kernel(*refs):
    for slab in range(ATTN_SLABS_PER_STEP):
        for step in _attn_steps(pl.program_id(2) == 0, slab, *refs):
            step()


def _attn_call(q, k, v):
    B, _, S, _ = q.shape
    sps = ATTN_SLABS_PER_STEP
    cur = pl.BlockSpec((1, sps, ATTN_TILE, LANES), lambda b, s, i: (b, s, i, 0))
    prev = pl.BlockSpec((1, sps, ATTN_TILE, LANES), lambda b, s, i: (b, s, jnp.maximum(i - 1, 0), 0))
    n4 = ATTN_TILE // MID_DILATION
    return pl.pallas_call(
        _attn_kernel,
        grid=(B, N_SLABS // sps, S // ATTN_TILE),
        in_specs=[cur, cur, prev, cur, prev],
        out_specs=pl.BlockSpec((1, ATTN_TILE, sps * LANES), lambda b, s, i: (b, i, s)),
        out_shape=jax.ShapeDtypeStruct((B, S, ATTN_WIDTH), BF16),
        scratch_shapes=[pltpu.VMEM((2, ATTN_BLOCK, 2 * ATTN_BLOCK), F32),
                        pltpu.VMEM((MID_DILATION, n4, LANES), F32),
                        pltpu.VMEM((MID_DILATION, 2 * n4, LANES), F32),
                        pltpu.VMEM((MID_DILATION, 2 * n4, LANES), F32),
                        pltpu.VMEM((ATTN_TILE, LANES), F32),
                        pltpu.VMEM((ATTN_TILE, LANES), F32),
                        pltpu.VMEM((ATTN_TILE, LANES), F32),
                        pltpu.VMEM((2 * ATTN_BLOCK, 2 * ATTN_BLOCK), F32),
                        pltpu.VMEM((2 * ATTN_BLOCK, 2 * ATTN_BLOCK), BF16)],
        compiler_params=pltpu.CompilerParams(
            dimension_semantics=("parallel", "parallel", "parallel"), vmem_limit_bytes=VMEM_LIMIT),
        name="dilated_attn",
    )(q, k, k, v, v)


def _hgrn_scan_matrices():
    C = HGRN_CHUNK
    t = np.arange(C)[:, None]
    u = np.arange(C)[None, :]
    mats = [np.broadcast_to(u <= t, (C, C))]
    for lvl in range(HGRN_SMALL_LEVELS):
        start = (t >> lvl) << lvl
        end = start + (1 << lvl) - 1
        is_query = ((t >> lvl) & 1) == 1
        mats.append(np.where(is_query, (u >= start) & (u <= t), (u > t) & (u <= end)))
    return np.concatenate(mats, axis=0).astype(np.float32)


def _hgrn_steps(first_step, hq_ref, hf_ref, hi_ref, hg_ref, loglb_ref, log1mlb_ref, onemlb_ref, nw_ref,
                scan_ref, o_ref, state_ref):
    C = HGRN_CHUNK

    @pl.when(first_step)
    def _():
        state_ref[...] = jnp.zeros_like(state_ref)

    t_idx = lax.broadcasted_iota(jnp.int32, (C, C), 0)
    s_idx = lax.broadcasted_iota(jnp.int32, (C, C), 1)
    differ = t_idx ^ s_idx
    level = jnp.full((C, C), -1, jnp.int32)
    for lvl in range(HGRN_LEVELS):
        level = level + ((differ >> lvl) > 0).astype(jnp.int32)
    level = jnp.where(t_idx < s_idx, -2, level)

    def nt_dot(a, b):
        return lax.dot_general(a, b, NT_DIMS, preferred_element_type=F32)

    items = [(c, h) for c in range(hq_ref.shape[2] // C) for h in range(hq_ref.shape[1])]
    gates, partial, start_states = {}, {}, {}

    def gate_step(c, head):
        rows = slice(c * C, (c + 1) * C)
        log_lb, log1m_lb, one_m_lb = loglb_ref[head], log1mlb_ref[head], onemlb_ref[head]
        fp = hf_ref[0, head, rows, :]
        hq = hq_ref[0, head, rows, :].astype(F32)

        t = fp * LOG2_E
        log_sig = jnp.minimum(t, 0.0) - jnp.log2(1.0 + jnp.exp2(-jnp.abs(t)))
        l2 = log1m_lb + log_sig
        a = jnp.maximum(log_lb, l2) + jnp.log2(1.0 + jnp.exp2(-jnp.abs(log_lb - l2)))
        k = one_m_lb * jnp.exp2(log_sig - t)
        q = hq / (1.0 + jnp.exp2(hq * (-LOG2_E)))

        a_hi = a.astype(BF16)
        rest = a - a_hi.astype(F32)
        a_mid = rest.astype(BF16)
        a_lo = (rest - a_mid.astype(F32)).astype(BF16)
        b3 = jnp.dot(scan_ref[:C, :], jnp.concatenate([a_hi, a_mid, a_lo], axis=1),
                     preferred_element_type=F32)
        b = b3[:, :LANES] + b3[:, LANES:2 * LANES] + b3[:, 2 * LANES:]
        small = jnp.dot(scan_ref[C:, :], jnp.concatenate([a_hi, a_mid], axis=1),
                        preferred_element_type=F32)
        small = small[:, :LANES] + small[:, LANES:]
        gates[c, head] = (q.astype(BF16), k.astype(BF16), b, small)

    def segment_sums(b, small, lvl):
        if lvl < HGRN_SMALL_LEVELS:
            return small[lvl * C:(lvl + 1) * C]
        w = 1 << lvl
        parts = []
        for pair in range(C // (2 * w)):
            lo = pair * 2 * w
            boundary = b[lo + w - 1:lo + w, :]
            parts += [boundary - b[lo:lo + w, :], b[lo + w:lo + 2 * w, :] - boundary]
        return jnp.concatenate(parts, axis=0)

    def score_step(c, head):
        q_bf, k_bf, b, small = gates[c, head]
        v = hi_ref[0, head, c * C:(c + 1) * C, :]
        attn = jnp.where(level == -1, nt_dot(q_bf, k_bf), 0.0)
        for lvl in range(HGRN_LEVELS):
            factor = jnp.exp2(segment_sums(b, small, lvl)).astype(BF16)
            attn = jnp.where(level == lvl, nt_dot(q_bf * factor, k_bf * factor), attn)
        o_intra = jnp.dot(attn.astype(BF16), v, preferred_element_type=F32)
        q_in = q_bf * jnp.exp2(b).astype(BF16)
        k_out = k_bf * jnp.exp2(b[C - 1:C, :] - b).astype(BF16)
        update = lax.dot_general(v, k_out, TN_DIMS, preferred_element_type=F32)
        partial[c, head] = (o_intra, q_in, update, jnp.exp2(b[C - 1:C, :]))

    def state_step():
        for c, head in items:
            _, _, update, chunk_decay = partial[c, head]
            state = state_ref[head]
            start_states[c, head] = state.astype(BF16)
            state_ref[head] = state * chunk_decay + update

    def output_step(c, head):
        o_intra, q_in, _, _ = partial[c, head]
        rows = slice(c * C, (c + 1) * C)
        hg = hg_ref[0, head, rows, :].astype(F32)
        o = o_intra + nt_dot(q_in, start_states[c, head])
        o = o * lax.rsqrt(jnp.mean(o * o, axis=-1, keepdims=True) + RMS_EPS)
        o = o * nw_ref[head] * (hg / (1.0 + jnp.exp2(hg * (-LOG2_E))))
        o_ref[0, rows, head * LANES:(head + 1) * LANES] = o.astype(o_ref.dtype)

    return ([functools.partial(gate_step, *item) for item in items]
            + [functools.partial(score_step, *item) for item in items]
            + [state_step]
            + [functools.partial(output_step, *item) for item in items])


def _hgrn_kernel(*refs):
    for step in _hgrn_steps(pl.program_id(1) == 0, *refs):
        step()


def _hgrn_call(hq, hf, hi, hg, log_lb, log1m_lb, one_m_lb, norm_w):
    B, H, S, _ = hq.shape
    tile = pl.BlockSpec((1, H, HGRN_TILE, LANES), lambda b, i: (b, 0, i, 0))
    per_head = _resident((H, 1, LANES))
    scan = jnp.asarray(_hgrn_scan_matrices(), dtype=BF16)
    return pl.pallas_call(
        _hgrn_kernel,
        grid=(B, S // HGRN_TILE),
        in_specs=[tile, tile, tile, tile, per_head, per_head, per_head, per_head,
                  _resident(scan.shape)],
        out_specs=pl.BlockSpec((1, HGRN_TILE, H * LANES), lambda b, i: (b, i, 0)),
        out_shape=jax.ShapeDtypeStruct((B, S, HGRN_WIDTH), BF16),
        scratch_shapes=[pltpu.VMEM((H, LANES, LANES), F32)],
        compiler_params=pltpu.CompilerParams(
            dimension_semantics=("parallel", "arbitrary"), vmem_limit_bytes=VMEM_LIMIT),
        name="hgrn2",
    )(hq, hf, hi, hg, log_lb, log1m_lb, one_m_lb, norm_w, scan)


def kernel(x, c, positions, ln_g, ln_b, ada_w, ada_b, ffn1_w_in, ffn1_w_out, ffn2_w_in, ffn2_w_out,
           mix_w_in, mix_w_out, hgrn_norm_w, hgrn_lb_logits):
    B, S, D = x.shape
    assert D == D_MODEL and S % ATTN_TILE == 0 and S % HGRN_TILE == 0
    assert S % (FFN_GROUPS * FFN_GROUP_ROWS) == 0 and S % ROW_TILE == 0

    lb_all = jnp.cumsum(jax.nn.softmax(hgrn_lb_logits.astype(F32), axis=0), axis=0)
    lb_all = (lb_all - lb_all[0:1]).reshape(DEPTH, N_SLABS, 1, LANES)
    log_lb, log1m_lb, one_m_lb = jnp.log(lb_all) * LOG2_E, jnp.log1p(-lb_all) * LOG2_E, 1.0 - lb_all
    norm_w = hgrn_norm_w.astype(F32).reshape(DEPTH, N_SLABS, 1, LANES)

    ada = _ada_call(c, ada_w, ada_b).reshape(DEPTH, B, N_SUBLAYERS, 3, 1, D)
    cos, sin_up, sin_dn = _rope_call(positions)

    for l in range(DEPTH):
        def mod(sub, kind, l=l):
            return ada[l, :, sub, kind]

        x = _ffn_call(x, mod(0, 0), mod(0, 1), mod(0, 2), ffn1_w_in[l].astype(BF16),
                      ffn1_w_out[l].astype(BF16), ln_g[l, 0], ln_b[l, 0])

        q, k, v, hq, hf, hi, hg = _proj_call(x, mod(1, 0), mod(1, 1), mix_w_in[l].astype(BF16),
                                             cos, sin_up, sin_dn)
        ao = _attn_call(q, k, v)
        go = _hgrn_call(hq, hf, hi, hg, log_lb[l], log1m_lb[l], one_m_lb[l], norm_w[l])
        x = _mix_out_ffn_call(x, ao, go, mod(1, 2), mix_w_out[l].astype(BF16), ln_g[l, 1], ln_b[l, 1],
                              mod(2, 0), mod(2, 1), mod(2, 2), ffn2_w_in[l].astype(BF16),
                              ffn2_w_out[l].astype(BF16), ln_g[l, 2], ln_b[l, 2])
    return x
```

```python
import functools
import math

import numpy as np
import jax
import jax.numpy as jnp
from jax import lax
from jax.experimental import pallas as pl
from jax.experimental.pallas import tpu as pltpu

D_MODEL = 1024
DEPTH = 2
N_SUBLAYERS = 3
ATTN_WIDTH = 512
HGRN_WIDTH = 512
HEAD_DIM = 64
LANES = 128
N_SLABS = ATTN_WIDTH // LANES
ROPE_THETA = 500000.0
ROPE_DIM = HEAD_DIM // 4
ROPE_HALF = ROPE_DIM // 2
D_FF = 2816
FFN_RES_WEIGHT = 0.5
DEEPNORM_ALPHA = (2 * DEPTH) ** 0.25
LN_EPS = 1e-5
RMS_EPS = 1e-6
IN_COLS = 3 * ATTN_WIDTH + 4 * HGRN_WIDTH
BAND = 128
ATTN_BLOCK = 128
MID_DILATION = 4
MAX_DILATION = 16
ATTN_TILE = ATTN_BLOCK * MAX_DILATION
ATTN_ROW_CHUNK = 32
HGRN_CHUNK = 128
HGRN_LEVELS = 7
HGRN_SMALL_LEVELS = 3
HGRN_TILE = 1024
ATTN_SLABS_PER_STEP = 2
ROW_TILE = 512
FFN_GROUP_ROWS = 256
FFN_GROUPS = 4
FF_CHUNK = 256
ADA_COLS = N_SUBLAYERS * 3 * D_MODEL
NEG_BIG = -1e30
LOG2_E = math.log2(math.e)
VMEM_LIMIT = 56 * 1024 * 1024

F32 = jnp.float32
BF16 = jnp.bfloat16
NT_DIMS = (((1,), (1,)), ((), ()))
TN_DIMS = (((0,), (0,)), ((), ()))


def _sigmoid(x):
    return 1.0 / (1.0 + jnp.exp(-x))


def _layer_norm(z, g, b):
    mu = jnp.mean(z, axis=-1, keepdims=True)
    zc = z - mu
    var = jnp.mean(zc * zc, axis=-1, keepdims=True)
    return zc * lax.rsqrt(var + LN_EPS) * g + b


def _resident(shape):
    zeros = (0,) * len(shape)
    return pl.BlockSpec(shape, lambda *_: zeros, pipeline_mode=pl.Buffered(1))


def _layer_resident(shape, layer):
    index = (layer,) + (0,) * len(shape)
    return pl.BlockSpec((None,) + tuple(shape), lambda *_: index, pipeline_mode=pl.Buffered(1))


def _ada_kernel(c_ref, w_ref, b_ref, o_ref):
    c = c_ref[...]
    cond = c * _sigmoid(c)
    o_ref[0] = jnp.dot(cond, w_ref[0], preferred_element_type=F32,
                       precision=lax.Precision.HIGHEST) + b_ref[0]


def _ada_call(c, ada_w, ada_b):
    B = c.shape[0]
    ncol = ADA_COLS // D_MODEL
    return pl.pallas_call(
        _ada_kernel,
        grid=(DEPTH, ncol),
        in_specs=[
            pl.BlockSpec((B, D_MODEL), lambda l, j: (0, 0)),
            pl.BlockSpec((1, D_MODEL, D_MODEL), lambda l, j: (l, 0, j)),
            pl.BlockSpec((1, 1, D_MODEL), lambda l, j: (l, 0, j)),
        ],
        out_specs=pl.BlockSpec((1, B, D_MODEL), lambda l, j: (l, 0, j)),
        out_shape=jax.ShapeDtypeStruct((DEPTH, B, ADA_COLS), F32),
        compiler_params=pltpu.CompilerParams(
            dimension_semantics=("parallel", "parallel"), vmem_limit_bytes=VMEM_LIMIT),
        name="ada",
    )(c, ada_w, ada_b.reshape(DEPTH, 1, ADA_COLS))


def _rope_kernel(pos_ref, cos_ref, sin_up_ref, sin_dn_ref):
    pos = pos_ref[0].astype(F32)
    lane = lax.broadcasted_iota(jnp.int32, (1, LANES), 1)
    in_head = lane & (HEAD_DIM - 1)
    freq = (in_head & (ROPE_HALF - 1)).astype(F32)
    inv_freq = jnp.exp(freq * (-math.log(ROPE_THETA) * 2.0 / ROPE_DIM))
    ang = pos * inv_freq
    cos = jnp.cos(ang)
    sin = jnp.sin(ang)
    first = in_head < ROPE_HALF
    second = jnp.logical_and(in_head >= ROPE_HALF, in_head < ROPE_DIM)
    cos_ref[0] = jnp.where(in_head < ROPE_DIM, cos, 1.0)
    sin_up_ref[0] = jnp.where(first, -sin, 0.0)
    sin_dn_ref[0] = jnp.where(second, sin, 0.0)


def _rope_call(positions):
    B, S = positions.shape
    rows = min(S, 1024)
    spec = pl.BlockSpec((1, rows, LANES), lambda b, i: (b, i, 0))
    shape = jax.ShapeDtypeStruct((B, S, LANES), F32)
    return pl.pallas_call(
        _rope_kernel,
        grid=(B, S // rows),
        in_specs=[pl.BlockSpec((1, rows, 1), lambda b, i: (b, i, 0))],
        out_specs=[spec, spec, spec],
        out_shape=[shape, shape, shape],
        compiler_params=pltpu.CompilerParams(
            dimension_semantics=("parallel", "parallel"), vmem_limit_bytes=VMEM_LIMIT),
        name="rope_tables",
    )(positions.reshape(B, S, 1))


def _ffn_rows(x, rows, shift_ref, scale_ref, gate_ref, win_ref, wout_ref, g_ref, b_ref, act_ref):
    h = (x * (1.0 + scale_ref[0]) + shift_ref[0]).astype(BF16)
    for j in range(D_FF // FF_CHUNK):
        lo = j * FF_CHUNK
        g = jnp.dot(h, win_ref[:, lo:lo + FF_CHUNK], preferred_element_type=F32)
        u = jnp.dot(h, win_ref[:, D_FF + lo:D_FF + lo + FF_CHUNK], preferred_element_type=F32)
        act_ref[rows, lo:lo + FF_CHUNK] = (g * _sigmoid(g) * u).astype(BF16)
    y = jnp.dot(act_ref[rows, :], wout_ref[...], preferred_element_type=F32)
    z = DEEPNORM_ALPHA * x + (FFN_RES_WEIGHT * (1.0 + gate_ref[0])) * y
    return _layer_norm(z, g_ref[...], b_ref[...])


def _row_groups(ref):
    return [slice(r * FFN_GROUP_ROWS, (r + 1) * FFN_GROUP_ROWS)
            for r in range(ref.shape[1] // FFN_GROUP_ROWS)]


def _ffn_kernel(x_ref, shift_ref, scale_ref, gate_ref, win_ref, wout_ref, g_ref, b_ref,
                o_ref, act_ref):
    for rows in _row_groups(x_ref):
        o_ref[0, rows, :] = _ffn_rows(x_ref[0, rows, :], rows, shift_ref, scale_ref, gate_ref,
                                      win_ref, wout_ref, g_ref, b_ref, act_ref)


def _mix_out_ffn_kernel(x_ref, ao_ref, go_ref, mgate_ref, wmix_ref, mg_ref, mb_ref,
                        shift_ref, scale_ref, gate_ref, win_ref, wout_ref, g_ref, b_ref,
                        o_ref, act_ref):
    for rows in _row_groups(x_ref):
        y = (jnp.dot(ao_ref[0, rows, :], wmix_ref[:ATTN_WIDTH, :], preferred_element_type=F32)
             + jnp.dot(go_ref[0, rows, :], wmix_ref[ATTN_WIDTH:, :], preferred_element_type=F32))
        z = DEEPNORM_ALPHA * x_ref[0, rows, :] + (1.0 + mgate_ref[0]) * y
        o_ref[0, rows, :] = _layer_norm(z, mg_ref[...], mb_ref[...])
    for rows in _row_groups(x_ref):
        o_ref[0, rows, :] = _ffn_rows(o_ref[0, rows, :], rows, shift_ref, scale_ref, gate_ref,
                                      win_ref, wout_ref, g_ref, b_ref, act_ref)


def _ffn_specs(D, layer):
    per_batch = pl.BlockSpec((1, 1, D), lambda b, i: (b, 0, 0))
    return [per_batch, per_batch, per_batch,
            _layer_resident((D, 2 * D_FF), layer), _layer_resident((D_FF, D), layer),
            _resident((1, D)), _resident((1, D))]


def _ffn_call(x, shift, scale, gate, w_in, w_out, ln_g, ln_b, layer):
    B, S, D = x.shape
    rows = FFN_GROUPS * FFN_GROUP_ROWS
    tile = pl.BlockSpec((1, rows, D), lambda b, i: (b, i, 0))
    return pl.pallas_call(
        _ffn_kernel,
        grid=(B, S // rows),
        in_specs=[tile] + _ffn_specs(D, layer),
        out_specs=tile,
        out_shape=jax.ShapeDtypeStruct((B, S, D), F32),
        scratch_shapes=[pltpu.VMEM((rows, D_FF), BF16)],
        compiler_params=pltpu.CompilerParams(
            dimension_semantics=("parallel", "parallel"), vmem_limit_bytes=VMEM_LIMIT),
        name="ffn",
    )(x, shift, scale, gate, w_in, w_out, ln_g.reshape(1, D), ln_b.reshape(1, D))


def _mix_out_ffn_call(x, ao, go, mix_gate, w_mix, mix_ln_g, mix_ln_b,
                      shift, scale, gate, w_in, w_out, ln_g, ln_b, layer):
    B, S, D = x.shape
    rows = FFN_GROUPS * FFN_GROUP_ROWS
    tile = pl.BlockSpec((1, rows, D), lambda b, i: (b, i, 0))
    half = pl.BlockSpec((1, rows, ATTN_WIDTH), lambda b, i: (b, i, 0))
    per_batch = pl.BlockSpec((1, 1, D), lambda b, i: (b, 0, 0))
    return pl.pallas_call(
        _mix_out_ffn_kernel,
        grid=(B, S // rows),
        in_specs=[tile, half, half, per_batch, _layer_resident((D, D), layer),
                  _resident((1, D)), _resident((1, D))] + _ffn_specs(D, layer),
        out_specs=tile,
        out_shape=jax.ShapeDtypeStruct((B, S, D), F32),
        scratch_shapes=[pltpu.VMEM((rows, D_FF), BF16)],
        compiler_params=pltpu.CompilerParams(
            dimension_semantics=("parallel", "parallel"), vmem_limit_bytes=VMEM_LIMIT),
        name="mix_out_ffn",
    )(x, ao, go, mix_gate, w_mix, mix_ln_g.reshape(1, D), mix_ln_b.reshape(1, D),
      shift, scale, gate, w_in, w_out, ln_g.reshape(1, D), ln_b.reshape(1, D))


def _proj_kernel(x_ref, shift_ref, scale_ref, w_ref, cos_ref, sup_ref, sdn_ref,
                 q_ref, k_ref, v_ref, hq_ref, hf_ref, hi_ref, hg_ref):
    x = x_ref[0]
    h = (x * (1.0 + scale_ref[0]) + shift_ref[0]).astype(BF16)
    cos, sup, sdn = cos_ref[0], sup_ref[0], sdn_ref[0]
    outs = (q_ref, k_ref, v_ref, hq_ref, hf_ref, hi_ref, hg_ref)
    for part, out in enumerate(outs):
        p = jnp.dot(h, w_ref[:, part * ATTN_WIDTH:(part + 1) * ATTN_WIDTH],
                    preferred_element_type=F32)
        for s in range(N_SLABS):
            slab = p[:, s * LANES:(s + 1) * LANES]
            if part < 2:
                slab = (slab * cos + pltpu.roll(slab, LANES - ROPE_HALF, 1) * sup
                        + pltpu.roll(slab, ROPE_HALF, 1) * sdn)
                if part == 0:
                    slab = slab * (HEAD_DIM ** -0.5 * LOG2_E)
            out[0, s] = slab.astype(out.dtype)


def _proj_call(x, shift, scale, w, cos, sup, sdn, layer):
    B, S, D = x.shape
    rows = min(S, ROW_TILE)
    tile = pl.BlockSpec((1, rows, D), lambda b, i: (b, i, 0))
    per_batch = pl.BlockSpec((1, 1, D), lambda b, i: (b, 0, 0))
    table = pl.BlockSpec((1, rows, LANES), lambda b, i: (b, i, 0))
    out_spec = pl.BlockSpec((1, N_SLABS, rows, LANES), lambda b, i: (b, 0, i, 0))
    dtypes = (F32, F32, F32, BF16, F32, BF16, BF16)
    return pl.pallas_call(
        _proj_kernel,
        grid=(B, S // rows),
        in_specs=[tile, per_batch, per_batch, _layer_resident((D, IN_COLS), layer), table, table, table],
        out_specs=[out_spec] * 7,
        out_shape=[jax.ShapeDtypeStruct((B, N_SLABS, S, LANES), dt) for dt in dtypes],
        compiler_params=pltpu.CompilerParams(
            dimension_semantics=("parallel", "parallel"), vmem_limit_bytes=VMEM_LIMIT),
        name="mix_proj",
    )(x, shift, scale, w, cos, sup, sdn)


def _attn_steps(first_tile, slab, q_ref, kc_ref, kp_ref, vc_ref, vp_ref, o_ref,
                bias_ref, q4_ref, k4_ref, v4_ref, m_ref, l_ref, acc_ref, s_ref, p_ref):
    lane = lax.broadcasted_iota(jnp.int32, (1, LANES), 1)
    head0 = lane < HEAD_DIM
    n4 = ATTN_TILE // MID_DILATION

    row = lax.broadcasted_iota(jnp.int32, (ATTN_BLOCK, 2 * ATTN_BLOCK), 0)
    col = lax.broadcasted_iota(jnp.int32, (ATTN_BLOCK, 2 * ATTN_BLOCK), 1)
    visible = jnp.logical_and(col >= row, col <= row + BAND)
    bias_ref[0] = jnp.where(visible, 0.0, NEG_BIG)
    no_prev = jnp.logical_and(visible, col >= jnp.where(first_tile, ATTN_BLOCK, 0))
    bias_ref[1] = jnp.where(no_prev, 0.0, NEG_BIG)

    for r in range(MID_DILATION):
        q4_ref[r] = q_ref[0, slab, pl.ds(r, n4, stride=MID_DILATION), :]
        k4_ref[r, :n4] = kp_ref[0, slab, pl.ds(r, n4, stride=MID_DILATION), :]
        k4_ref[r, n4:] = kc_ref[0, slab, pl.ds(r, n4, stride=MID_DILATION), :]
        v4_ref[r, :n4] = vp_ref[0, slab, pl.ds(r, n4, stride=MID_DILATION), :]
        v4_ref[r, n4:] = vc_ref[0, slab, pl.ds(r, n4, stride=MID_DILATION), :]

    n_chunks = ATTN_BLOCK // ATTN_ROW_CHUNK

    def score_stage(q, k2, which_bias):
        zero = jnp.zeros_like(q)
        qq = jnp.concatenate([jnp.where(head0, q, zero), jnp.where(head0, zero, q)], axis=0)
        scores = lax.dot_general(qq.astype(BF16), k2.astype(BF16), NT_DIMS, preferred_element_type=F32)
        row_max = [[None] * n_chunks, [None] * n_chunks]
        for head in range(2):
            for c in range(n_chunks):
                lo = c * ATTN_ROW_CHUNK
                rows = slice(head * ATTN_BLOCK + lo, head * ATTN_BLOCK + lo + ATTN_ROW_CHUNK)
                s = scores[rows, :] + bias_ref[which_bias, lo:lo + ATTN_ROW_CHUNK, :]
                s_ref[rows, :] = s
                row_max[head][c] = jnp.max(s, axis=-1, keepdims=True)
        return row_max

    def value_stage(row_max, v2):
        for head in range(2):
            for c in range(n_chunks):
                lo = head * ATTN_BLOCK + c * ATTN_ROW_CHUNK
                p_ref[lo:lo + ATTN_ROW_CHUNK, :] = jnp.exp2(
                    s_ref[lo:lo + ATTN_ROW_CHUNK, :] - row_max[head][c]).astype(BF16)
        v_ones = jnp.concatenate([v2.astype(BF16), jnp.ones((2 * ATTN_BLOCK, LANES), BF16)], axis=1)
        pv = jnp.dot(p_ref[...], v_ones, preferred_element_type=F32)
        acc_b = jnp.where(head0, pv[:ATTN_BLOCK, :LANES], pv[ATTN_BLOCK:, :LANES])
        l_b = jnp.where(head0, pv[:ATTN_BLOCK, LANES:], pv[ATTN_BLOCK:, LANES:])
        m_b = jnp.concatenate([jnp.where(head0, row_max[0][c], row_max[1][c]) for c in range(n_chunks)],
                              axis=0)
        return m_b, l_b, acc_b

    def merged(rows, m_b, l_b, acc_b):
        m_old = m_ref[rows, :]
        m_new = jnp.maximum(m_old, m_b)
        w_old = jnp.exp2(m_old - m_new)
        w_blk = jnp.exp2(m_b - m_new)
        return (m_new, w_old * l_ref[rows, :] + w_blk * l_b, w_old * acc_ref[rows, :] + w_blk * acc_b)

    blocks = []

    for r16 in range(MAX_DILATION):
        r4, j = r16 % MID_DILATION, r16 // MID_DILATION

        def load_qk(r4=r4, j=j):
            return (q4_ref[r4, pl.ds(j, ATTN_BLOCK, stride=MID_DILATION), :],
                    k4_ref[r4, pl.ds(j, 2 * ATTN_BLOCK, stride=MID_DILATION), :])

        def load_v(r4=r4, j=j):
            return v4_ref[r4, pl.ds(j, 2 * ATTN_BLOCK, stride=MID_DILATION), :]

        def consume(m_b, l_b, acc_b, r16=r16):
            rows = pl.ds(r16, ATTN_BLOCK, stride=MAX_DILATION)
            m_ref[rows, :] = m_b
            l_ref[rows, :] = l_b
            acc_ref[rows, :] = acc_b

        blocks.append((load_qk, 1, load_v, consume))

    for r in range(MID_DILATION):
        for nb in range(n4 // ATTN_BLOCK):
            lo = nb * ATTN_BLOCK
            keys = slice(n4 - ATTN_BLOCK + lo, n4 + ATTN_BLOCK + lo)

            def load_qk(r=r, lo=lo, keys=keys):
                return q4_ref[r, lo:lo + ATTN_BLOCK, :], k4_ref[r, keys, :]

            def load_v(r=r, keys=keys):
                return v4_ref[r, keys, :]

            def consume(m_b, l_b, acc_b, r=r, lo=lo):
                rows = pl.ds(r + MID_DILATION * lo, ATTN_BLOCK, stride=MID_DILATION)
                m_new, l_new, acc_new = merged(rows, m_b, l_b, acc_b)
                m_ref[rows, :] = m_new
                l_ref[rows, :] = l_new
                acc_ref[rows, :] = acc_new

            blocks.append((load_qk, 1 if nb == 0 else 0, load_v, consume))

    for nb in range(ATTN_TILE // ATTN_BLOCK):
        lo = nb * ATTN_BLOCK

        def with_prev(cur_ref, prev_ref, lo=lo):
            if lo == 0:
                return jnp.concatenate([prev_ref[0, slab, ATTN_TILE - ATTN_BLOCK:, :],
                                        cur_ref[0, slab, :ATTN_BLOCK, :]], axis=0)
            return cur_ref[0, slab, lo - ATTN_BLOCK:lo + ATTN_BLOCK, :]

        def load_qk(lo=lo, with_prev=with_prev):
            return q_ref[0, slab, lo:lo + ATTN_BLOCK, :], with_prev(kc_ref, kp_ref)

        def load_v(with_prev=with_prev):
            return with_prev(vc_ref, vp_ref)

        def consume(m_b, l_b, acc_b, lo=lo):
            _, l_new, acc_new = merged(pl.ds(lo, ATTN_BLOCK), m_b, l_b, acc_b)
            o_ref[0, lo:lo + ATTN_BLOCK, slab * LANES:(slab + 1) * LANES] = (
                acc_new / l_new).astype(o_ref.dtype)

        blocks.append((load_qk, 1 if nb == 0 else 0, load_v, consume))

    def run_block(load_qk, which_bias, load_v, consume):
        row_max = score_stage(*load_qk(), which_bias)
        consume(*value_stage(row_max, load_v()))

    return [functools.partial(run_block, *block) for block in blocks]


def _attn_kernel(*refs):
    for slab in range(ATTN_SLABS_PER_STEP):
        for step in _attn_steps(pl.program_id(2) == 0, slab, *refs):
            step()


def _attn_call(q, k, v):
    B, _, S, _ = q.shape
    sps = ATTN_SLABS_PER_STEP
    cur = pl.BlockSpec((1, sps, ATTN_TILE, LANES), lambda b, s, i: (b, s, i, 0))
    prev = pl.BlockSpec((1, sps, ATTN_TILE, LANES), lambda b, s, i: (b, s, jnp.maximum(i - 1, 0), 0))
    n4 = ATTN_TILE // MID_DILATION
    return pl.pallas_call(
        _attn_kernel,
        grid=(B, N_SLABS // sps, S // ATTN_TILE),
        in_specs=[cur, cur, prev, cur, prev],
        out_specs=pl.BlockSpec((1, ATTN_TILE, sps * LANES), lambda b, s, i: (b, i, s)),
        out_shape=jax.ShapeDtypeStruct((B, S, ATTN_WIDTH), BF16),
        scratch_shapes=[pltpu.VMEM((2, ATTN_BLOCK, 2 * ATTN_BLOCK), F32),
                        pltpu.VMEM((MID_DILATION, n4, LANES), F32),
                        pltpu.VMEM((MID_DILATION, 2 * n4, LANES), F32),
                        pltpu.VMEM((MID_DILATION, 2 * n4, LANES), F32),
                        pltpu.VMEM((ATTN_TILE, LANES), F32),
                        pltpu.VMEM((ATTN_TILE, LANES), F32),
                        pltpu.VMEM((ATTN_TILE, LANES), F32),
                        pltpu.VMEM((2 * ATTN_BLOCK, 2 * ATTN_BLOCK), F32),
                        pltpu.VMEM((2 * ATTN_BLOCK, 2 * ATTN_BLOCK), BF16)],
        compiler_params=pltpu.CompilerParams(
            dimension_semantics=("parallel", "parallel", "parallel"), vmem_limit_bytes=VMEM_LIMIT),
        name="dilated_attn",
    )(q, k, k, v, v)


def _hgrn_scan_matrices():
    C = HGRN_CHUNK
    t = np.arange(C)[:, None]
    u = np.arange(C)[None, :]
    mats = [np.broadcast_to(u <= t, (C, C))]
    for lvl in range(HGRN_SMALL_LEVELS):
        start = (t >> lvl) << lvl
        end = start + (1 << lvl) - 1
        is_query = ((t >> lvl) & 1) == 1
        mats.append(np.where(is_query, (u >= start) & (u <= t), (u > t) & (u <= end)))
    return np.concatenate(mats, axis=0).astype(np.float32)


def _hgrn_steps(first_step, hq_ref, hf_ref, hi_ref, hg_ref, loglb_ref, log1mlb_ref, onemlb_ref, nw_ref,
                scan_ref, o_ref, state_ref):
    C = HGRN_CHUNK

    @pl.when(first_step)
    def _():
        state_ref[...] = jnp.zeros_like(state_ref)

    t_idx = lax.broadcasted_iota(jnp.int32, (C, C), 0)
    s_idx = lax.broadcasted_iota(jnp.int32, (C, C), 1)
    differ = t_idx ^ s_idx
    level = jnp.full((C, C), -1, jnp.int32)
    for lvl in range(HGRN_LEVELS):
        level = level + ((differ >> lvl) > 0).astype(jnp.int32)
    level = jnp.where(t_idx < s_idx, -2, level)

    def nt_dot(a, b):
        return lax.dot_general(a, b, NT_DIMS, preferred_element_type=F32)

    items = [(c, h) for c in range(hq_ref.shape[2] // C) for h in range(hq_ref.shape[1])]
    gates, partial, start_states = {}, {}, {}

    def gate_step(c, head):
        rows = slice(c * C, (c + 1) * C)
        log_lb, log1m_lb, one_m_lb = loglb_ref[head], log1mlb_ref[head], onemlb_ref[head]
        fp = hf_ref[0, head, rows, :]
        hq = hq_ref[0, head, rows, :].astype(F32)

        t = fp * LOG2_E
        log_sig = jnp.minimum(t, 0.0) - jnp.log2(1.0 + jnp.exp2(-jnp.abs(t)))
        l2 = log1m_lb + log_sig
        a = jnp.maximum(log_lb, l2) + jnp.log2(1.0 + jnp.exp2(-jnp.abs(log_lb - l2)))
        k = one_m_lb * jnp.exp2(log_sig - t)
        q = hq / (1.0 + jnp.exp2(hq * (-LOG2_E)))

        a_hi = a.astype(BF16)
        rest = a - a_hi.astype(F32)
        a_mid = rest.astype(BF16)
        a_lo = (rest - a_mid.astype(F32)).astype(BF16)
        b3 = jnp.dot(scan_ref[:C, :], jnp.concatenate([a_hi, a_mid, a_lo], axis=1),
                     preferred_element_type=F32)
        b = b3[:, :LANES] + b3[:, LANES:2 * LANES] + b3[:, 2 * LANES:]
        small = jnp.dot(scan_ref[C:, :], jnp.concatenate([a_hi, a_mid], axis=1),
                        preferred_element_type=F32)
        small = small[:, :LANES] + small[:, LANES:]
        gates[c, head] = (q.astype(BF16), k.astype(BF16), b, small)

    def segment_sums(b, small, lvl):
        if lvl < HGRN_SMALL_LEVELS:
            return small[lvl * C:(lvl + 1) * C]
        w = 1 << lvl
        parts = []
        for pair in range(C // (2 * w)):
            lo = pair * 2 * w
            boundary = b[lo + w - 1:lo + w, :]
            parts += [boundary - b[lo:lo + w, :], b[lo + w:lo + 2 * w, :] - boundary]
        return jnp.concatenate(parts, axis=0)

    def score_step(c, head):
        q_bf, k_bf, b, small = gates[c, head]
        v = hi_ref[0, head, c * C:(c + 1) * C, :]
        attn = jnp.where(level == -1, nt_dot(q_bf, k_bf), 0.0)
        for lvl in range(HGRN_LEVELS):
            factor = jnp.exp2(segment_sums(b, small, lvl)).astype(BF16)
            attn = jnp.where(level == lvl, nt_dot(q_bf * factor, k_bf * factor), attn)
        o_intra = jnp.dot(attn.astype(BF16), v, preferred_element_type=F32)
        q_in = q_bf * jnp.exp2(b).astype(BF16)
        k_out = k_bf * jnp.exp2(b[C - 1:C, :] - b).astype(BF16)
        update = lax.dot_general(v, k_out, TN_DIMS, preferred_element_type=F32)
        partial[c, head] = (o_intra, q_in, update, jnp.exp2(b[C - 1:C, :]))

    def state_step():
        for c, head in items:
            _, _, update, chunk_decay = partial[c, head]
            state = state_ref[head]
            start_states[c, head] = state.astype(BF16)
            state_ref[head] = state * chunk_decay + update

    def output_step(c, head):
        o_intra, q_in, _, _ = partial[c, head]
        rows = slice(c * C, (c + 1) * C)
        hg = hg_ref[0, head, rows, :].astype(F32)
        o = o_intra + nt_dot(q_in, start_states[c, head])
        o = o * lax.rsqrt(jnp.mean(o * o, axis=-1, keepdims=True) + RMS_EPS)
        o = o * nw_ref[head] * (hg / (1.0 + jnp.exp2(hg * (-LOG2_E))))
        o_ref[0, rows, head * LANES:(head + 1) * LANES] = o.astype(o_ref.dtype)

    return ([functools.partial(gate_step, *item) for item in items]
            + [functools.partial(score_step, *item) for item in items]
            + [state_step]
            + [functools.partial(output_step, *item) for item in items])


def _hgrn_kernel(*refs):
    for step in _hgrn_steps(pl.program_id(1) == 0, *refs):
        step()


def _hgrn_call(hq, hf, hi, hg, log_lb, log1m_lb, one_m_lb, norm_w):
    B, H, S, _ = hq.shape
    tile = pl.BlockSpec((1, H, HGRN_TILE, LANES), lambda b, i: (b, 0, i, 0))
    per_head = _resident((H, 1, LANES))
    scan = jnp.asarray(_hgrn_scan_matrices(), dtype=BF16)
    return pl.pallas_call(
        _hgrn_kernel,
        grid=(B, S // HGRN_TILE),
        in_specs=[tile, tile, tile, tile, per_head, per_head, per_head, per_head,
                  _resident(scan.shape)],
        out_specs=pl.BlockSpec((1, HGRN_TILE, H * LANES), lambda b, i: (b, i, 0)),
        out_shape=jax.ShapeDtypeStruct((B, S, HGRN_WIDTH), BF16),
        scratch_shapes=[pltpu.VMEM((H, LANES, LANES), F32)],
        compiler_params=pltpu.CompilerParams(
            dimension_semantics=("parallel", "arbitrary"), vmem_limit_bytes=VMEM_LIMIT),
        name="hgrn2",
    )(hq, hf, hi, hg, log_lb, log1m_lb, one_m_lb, norm_w, scan)


def kernel(x, c, positions, ln_g, ln_b, ada_w, ada_b, ffn1_w_in, ffn1_w_out, ffn2_w_in, ffn2_w_out,
           mix_w_in, mix_w_out, hgrn_norm_w, hgrn_lb_logits):
    B, S, D = x.shape
    assert D == D_MODEL and S % ATTN_TILE == 0 and S % HGRN_TILE == 0
    assert S % (FFN_GROUPS * FFN_GROUP_ROWS) == 0 and S % ROW_TILE == 0

    lb_all = jnp.cumsum(jax.nn.softmax(hgrn_lb_logits.astype(F32), axis=0), axis=0)
    lb_all = (lb_all - lb_all[0:1]).reshape(DEPTH, N_SLABS, 1, LANES)
    log_lb, log1m_lb, one_m_lb = jnp.log(lb_all) * LOG2_E, jnp.log1p(-lb_all) * LOG2_E, 1.0 - lb_all
    norm_w = hgrn_norm_w.astype(F32).reshape(DEPTH, N_SLABS, 1, LANES)

    ada = _ada_call(c, ada_w, ada_b).reshape(DEPTH, B, N_SUBLAYERS, 3, 1, D)
    cos, sin_up, sin_dn = _rope_call(positions)

    ffn1_in, ffn1_out, ffn2_in, ffn2_out, mix_in, mix_out = (
        w.astype(BF16) for w in (ffn1_w_in, ffn1_w_out, ffn2_w_in, ffn2_w_out, mix_w_in, mix_w_out))

    for l in range(DEPTH):
        def mod(sub, kind, l=l):
            return ada[l, :, sub, kind]

        x = _ffn_call(x, mod(0, 0), mod(0, 1), mod(0, 2), ffn1_in, ffn1_out, ln_g[l, 0], ln_b[l, 0], l)

        q, k, v, hq, hf, hi, hg = _proj_call(x, mod(1, 0), mod(1, 1), mix_in, cos, sin_up, sin_dn, l)
        ao = _attn_call(q, k, v)
        go = _hgrn_call(hq, hf, hi, hg, log_lb[l], log1m_lb[l], one_m_lb[l], norm_w[l])
        x = _mix_out_ffn_call(x, ao, go, mod(1, 2), mix_out, ln_g[l, 1], ln_b[l, 1],
                              mod(2, 0), mod(2, 1), mod(2, 2), ffn2_in, ffn2_out,
                              ln_g[l, 2], ln_b[l, 2], l)
    return x
```

```python
import functools
import math

import numpy as np
import jax
import jax.numpy as jnp
from jax import lax
from jax.experimental import pallas as pl
from jax.experimental.pallas import tpu as pltpu

D_MODEL = 1024
DEPTH = 2
N_SUBLAYERS = 3
ATTN_WIDTH = 512
HGRN_WIDTH = 512
HEAD_DIM = 64
LANES = 128
N_SLABS = ATTN_WIDTH // LANES
ROPE_THETA = 500000.0
ROPE_DIM = HEAD_DIM // 4
ROPE_HALF = ROPE_DIM // 2
D_FF = 2816
FFN_RES_WEIGHT = 0.5
DEEPNORM_ALPHA = (2 * DEPTH) ** 0.25
LN_EPS = 1e-5
RMS_EPS = 1e-6
IN_COLS = 3 * ATTN_WIDTH + 4 * HGRN_WIDTH
BAND = 128
ATTN_BLOCK = 128
MID_DILATION = 4
MAX_DILATION = 16
ATTN_TILE = ATTN_BLOCK * MAX_DILATION
ATTN_ROW_CHUNK = 32
HGRN_CHUNK = 128
HGRN_LEVELS = 7
HGRN_SMALL_LEVELS = 3
HGRN_TILE = 1024
ATTN_SLABS_PER_STEP = 2
ROW_TILE = 1024
FFN_GROUP_ROWS = 256
FFN_GROUPS = 4
FF_CHUNK = 256
ADA_COLS = N_SUBLAYERS * 3 * D_MODEL
NEG_BIG = -1e30
LOG2_E = math.log2(math.e)
VMEM_LIMIT = 56 * 1024 * 1024

F32 = jnp.float32
BF16 = jnp.bfloat16
NT_DIMS = (((1,), (1,)), ((), ()))
TN_DIMS = (((0,), (0,)), ((), ()))


def _sigmoid(x):
    return 1.0 / (1.0 + jnp.exp(-x))


def _layer_norm(z, g, b):
    mu = jnp.mean(z, axis=-1, keepdims=True)
    zc = z - mu
    var = jnp.mean(zc * zc, axis=-1, keepdims=True)
    return zc * lax.rsqrt(var + LN_EPS) * g + b


def _resident(shape):
    zeros = (0,) * len(shape)
    return pl.BlockSpec(shape, lambda *_: zeros, pipeline_mode=pl.Buffered(1))


def _layer_resident(shape, layer):
    index = (layer,) + (0,) * len(shape)
    return pl.BlockSpec((None,) + tuple(shape), lambda *_: index, pipeline_mode=pl.Buffered(1))


def _ada_kernel(c_ref, w_ref, b_ref, o_ref):
    c = c_ref[...]
    cond = c * _sigmoid(c)
    o_ref[0] = jnp.dot(cond, w_ref[0], preferred_element_type=F32,
                       precision=lax.Precision.HIGHEST) + b_ref[0]


def _ada_call(c, ada_w, ada_b):
    B = c.shape[0]
    ncol = ADA_COLS // D_MODEL
    return pl.pallas_call(
        _ada_kernel,
        grid=(DEPTH, ncol),
        in_specs=[
            pl.BlockSpec((B, D_MODEL), lambda l, j: (0, 0)),
            pl.BlockSpec((1, D_MODEL, D_MODEL), lambda l, j: (l, 0, j)),
            pl.BlockSpec((1, 1, D_MODEL), lambda l, j: (l, 0, j)),
        ],
        out_specs=pl.BlockSpec((1, B, D_MODEL), lambda l, j: (l, 0, j)),
        out_shape=jax.ShapeDtypeStruct((DEPTH, B, ADA_COLS), F32),
        compiler_params=pltpu.CompilerParams(
            dimension_semantics=("parallel", "parallel"), vmem_limit_bytes=VMEM_LIMIT),
        name="ada",
    )(c, ada_w, ada_b.reshape(DEPTH, 1, ADA_COLS))


def _rope_kernel(pos_ref, cos_ref, sin_up_ref, sin_dn_ref):
    pos = pos_ref[0].astype(F32)
    lane = lax.broadcasted_iota(jnp.int32, (1, LANES), 1)
    in_head = lane & (HEAD_DIM - 1)
    freq = (in_head & (ROPE_HALF - 1)).astype(F32)
    inv_freq = jnp.exp(freq * (-math.log(ROPE_THETA) * 2.0 / ROPE_DIM))
    ang = pos * inv_freq
    cos = jnp.cos(ang)
    sin = jnp.sin(ang)
    first = in_head < ROPE_HALF
    second = jnp.logical_and(in_head >= ROPE_HALF, in_head < ROPE_DIM)
    cos_ref[0] = jnp.where(in_head < ROPE_DIM, cos, 1.0)
    sin_up_ref[0] = jnp.where(first, -sin, 0.0)
    sin_dn_ref[0] = jnp.where(second, sin, 0.0)


def _rope_call(positions):
    B, S = positions.shape
    rows = min(S, 1024)
    spec = pl.BlockSpec((1, rows, LANES), lambda b, i: (b, i, 0))
    shape = jax.ShapeDtypeStruct((B, S, LANES), F32)
    return pl.pallas_call(
        _rope_kernel,
        grid=(B, S // rows),
        in_specs=[pl.BlockSpec((1, rows, 1), lambda b, i: (b, i, 0))],
        out_specs=[spec, spec, spec],
        out_shape=[shape, shape, shape],
        compiler_params=pltpu.CompilerParams(
            dimension_semantics=("parallel", "parallel"), vmem_limit_bytes=VMEM_LIMIT),
        name="rope_tables",
    )(positions.reshape(B, S, 1))


def _ffn_rows(x, rows, shift_ref, scale_ref, gate_ref, win_ref, wout_ref, g_ref, b_ref, act_ref):
    h = (x * (1.0 + scale_ref[0]) + shift_ref[0]).astype(BF16)
    for j in range(D_FF // FF_CHUNK):
        lo = j * FF_CHUNK
        g = jnp.dot(h, win_ref[:, lo:lo + FF_CHUNK], preferred_element_type=F32)
        u = jnp.dot(h, win_ref[:, D_FF + lo:D_FF + lo + FF_CHUNK], preferred_element_type=F32)
        act_ref[rows, lo:lo + FF_CHUNK] = (g * _sigmoid(g) * u).astype(BF16)
    y = jnp.dot(act_ref[rows, :], wout_ref[...], preferred_element_type=F32)
    z = DEEPNORM_ALPHA * x + (FFN_RES_WEIGHT * (1.0 + gate_ref[0])) * y
    return _layer_norm(z, g_ref[...], b_ref[...])


def _row_groups(ref):
    return [slice(r * FFN_GROUP_ROWS, (r + 1) * FFN_GROUP_ROWS)
            for r in range(ref.shape[1] // FFN_GROUP_ROWS)]


def _ffn_kernel(x_ref, shift_ref, scale_ref, gate_ref, win_ref, wout_ref, g_ref, b_ref,
                o_ref, act_ref):
    for rows in _row_groups(x_ref):
        o_ref[0, rows, :] = _ffn_rows(x_ref[0, rows, :], rows, shift_ref, scale_ref, gate_ref,
                                      win_ref, wout_ref, g_ref, b_ref, act_ref)


def _mix_out_ffn_kernel(x_ref, ao_ref, go_ref, mgate_ref, wmix_ref, mg_ref, mb_ref,
                        shift_ref, scale_ref, gate_ref, win_ref, wout_ref, g_ref, b_ref,
                        o_ref, act_ref):
    for rows in _row_groups(x_ref):
        y = (jnp.dot(ao_ref[0, rows, :], wmix_ref[:ATTN_WIDTH, :], preferred_element_type=F32)
             + jnp.dot(go_ref[0, rows, :], wmix_ref[ATTN_WIDTH:, :], preferred_element_type=F32))
        z = DEEPNORM_ALPHA * x_ref[0, rows, :] + (1.0 + mgate_ref[0]) * y
        o_ref[0, rows, :] = _layer_norm(z, mg_ref[...], mb_ref[...])
    for rows in _row_groups(x_ref):
        o_ref[0, rows, :] = _ffn_rows(o_ref[0, rows, :], rows, shift_ref, scale_ref, gate_ref,
                                      win_ref, wout_ref, g_ref, b_ref, act_ref)


def _ffn_specs(D, layer):
    per_batch = pl.BlockSpec((1, 1, D), lambda b, i: (b, 0, 0))
    return [per_batch, per_batch, per_batch,
            _layer_resident((D, 2 * D_FF), layer), _layer_resident((D_FF, D), layer),
            _resident((1, D)), _resident((1, D))]


def _ffn_call(x, shift, scale, gate, w_in, w_out, ln_g, ln_b, layer):
    B, S, D = x.shape
    rows = FFN_GROUPS * FFN_GROUP_ROWS
    tile = pl.BlockSpec((1, rows, D), lambda b, i: (b, i, 0))
    return pl.pallas_call(
        _ffn_kernel,
        grid=(B, S // rows),
        in_specs=[tile] + _ffn_specs(D, layer),
        out_specs=tile,
        out_shape=jax.ShapeDtypeStruct((B, S, D), F32),
        scratch_shapes=[pltpu.VMEM((rows, D_FF), BF16)],
        compiler_params=pltpu.CompilerParams(
            dimension_semantics=("parallel", "parallel"), vmem_limit_bytes=VMEM_LIMIT),
        name="ffn",
    )(x, shift, scale, gate, w_in, w_out, ln_g.reshape(1, D), ln_b.reshape(1, D))


def _mix_out_ffn_call(x, ao, go, mix_gate, w_mix, mix_ln_g, mix_ln_b,
                      shift, scale, gate, w_in, w_out, ln_g, ln_b, layer):
    B, S, D = x.shape
    rows = FFN_GROUPS * FFN_GROUP_ROWS
    tile = pl.BlockSpec((1, rows, D), lambda b, i: (b, i, 0))
    half = pl.BlockSpec((1, rows, ATTN_WIDTH), lambda b, i: (b, i, 0))
    per_batch = pl.BlockSpec((1, 1, D), lambda b, i: (b, 0, 0))
    return pl.pallas_call(
        _mix_out_ffn_kernel,
        grid=(B, S // rows),
        in_specs=[tile, half, half, per_batch, _layer_resident((D, D), layer),
                  _resident((1, D)), _resident((1, D))] + _ffn_specs(D, layer),
        out_specs=tile,
        out_shape=jax.ShapeDtypeStruct((B, S, D), F32),
        scratch_shapes=[pltpu.VMEM((rows, D_FF), BF16)],
        compiler_params=pltpu.CompilerParams(
            dimension_semantics=("parallel", "parallel"), vmem_limit_bytes=VMEM_LIMIT),
        name="mix_out_ffn",
    )(x, ao, go, mix_gate, w_mix, mix_ln_g.reshape(1, D), mix_ln_b.reshape(1, D),
      shift, scale, gate, w_in, w_out, ln_g.reshape(1, D), ln_b.reshape(1, D))


def _proj_kernel(x_ref, shift_ref, scale_ref, w_ref, cos_ref, sup_ref, sdn_ref,
                 q_ref, k_ref, v_ref, hq_ref, hf_ref, hi_ref, hg_ref):
    x = x_ref[0]
    h = (x * (1.0 + scale_ref[0]) + shift_ref[0]).astype(BF16)
    cos, sup, sdn = cos_ref[0], sup_ref[0], sdn_ref[0]
    outs = (q_ref, k_ref, v_ref, hq_ref, hf_ref, hi_ref, hg_ref)
    for part, out in enumerate(outs):
        p = jnp.dot(h, w_ref[:, part * ATTN_WIDTH:(part + 1) * ATTN_WIDTH],
                    preferred_element_type=F32)
        for s in range(N_SLABS):
            slab = p[:, s * LANES:(s + 1) * LANES]
            if part < 2:
                slab = (slab * cos + pltpu.roll(slab, LANES - ROPE_HALF, 1) * sup
                        + pltpu.roll(slab, ROPE_HALF, 1) * sdn)
                if part == 0:
                    slab = slab * (HEAD_DIM ** -0.5 * LOG2_E)
            out[0, s] = slab.astype(out.dtype)


def _proj_call(x, shift, scale, w, cos, sup, sdn, layer):
    B, S, D = x.shape
    rows = min(S, ROW_TILE)
    tile = pl.BlockSpec((1, rows, D), lambda b, i: (b, i, 0))
    per_batch = pl.BlockSpec((1, 1, D), lambda b, i: (b, 0, 0))
    table = pl.BlockSpec((1, rows, LANES), lambda b, i: (b, i, 0))
    out_spec = pl.BlockSpec((1, N_SLABS, rows, LANES), lambda b, i: (b, 0, i, 0))
    dtypes = (F32, F32, F32, BF16, F32, BF16, BF16)
    return pl.pallas_call(
        _proj_kernel,
        grid=(B, S // rows),
        in_specs=[tile, per_batch, per_batch, _layer_resident((D, IN_COLS), layer), table, table, table],
        out_specs=[out_spec] * 7,
        out_shape=[jax.ShapeDtypeStruct((B, N_SLABS, S, LANES), dt) for dt in dtypes],
        compiler_params=pltpu.CompilerParams(
            dimension_semantics=("parallel", "parallel"), vmem_limit_bytes=VMEM_LIMIT),
        name="mix_proj",
    )(x, shift, scale, w, cos, sup, sdn)


def _attn_steps(first_tile, slab, q_ref, kc_ref, kp_ref, vc_ref, vp_ref, o_ref,
                bias_ref, q4_ref, k4_ref, v4_ref, m_ref, l_ref, acc_ref, s_ref, p_ref):
    lane = lax.broadcasted_iota(jnp.int32, (1, LANES), 1)
    head0 = lane < HEAD_DIM
    n4 = ATTN_TILE // MID_DILATION

    row = lax.broadcasted_iota(jnp.int32, (ATTN_BLOCK, 2 * ATTN_BLOCK), 0)
    col = lax.broadcasted_iota(jnp.int32, (ATTN_BLOCK, 2 * ATTN_BLOCK), 1)
    visible = jnp.logical_and(col >= row, col <= row + BAND)
    bias_ref[0] = jnp.where(visible, 0.0, NEG_BIG)
    no_prev = jnp.logical_and(visible, col >= jnp.where(first_tile, ATTN_BLOCK, 0))
    bias_ref[1] = jnp.where(no_prev, 0.0, NEG_BIG)

    for r in range(MID_DILATION):
        q4_ref[r] = q_ref[0, slab, pl.ds(r, n4, stride=MID_DILATION), :]
        k4_ref[r, :n4] = kp_ref[0, slab, pl.ds(r, n4, stride=MID_DILATION), :]
        k4_ref[r, n4:] = kc_ref[0, slab, pl.ds(r, n4, stride=MID_DILATION), :]
        v4_ref[r, :n4] = vp_ref[0, slab, pl.ds(r, n4, stride=MID_DILATION), :]
        v4_ref[r, n4:] = vc_ref[0, slab, pl.ds(r, n4, stride=MID_DILATION), :]

    n_chunks = ATTN_BLOCK // ATTN_ROW_CHUNK

    def score_stage(q, k2, which_bias):
        zero = jnp.zeros_like(q)
        qq = jnp.concatenate([jnp.where(head0, q, zero), jnp.where(head0, zero, q)], axis=0)
        scores = lax.dot_general(qq.astype(BF16), k2.astype(BF16), NT_DIMS, preferred_element_type=F32)
        row_max = [[None] * n_chunks, [None] * n_chunks]
        for head in range(2):
            for c in range(n_chunks):
                lo = c * ATTN_ROW_CHUNK
                rows = slice(head * ATTN_BLOCK + lo, head * ATTN_BLOCK + lo + ATTN_ROW_CHUNK)
                s = scores[rows, :] + bias_ref[which_bias, lo:lo + ATTN_ROW_CHUNK, :]
                s_ref[rows, :] = s
                row_max[head][c] = jnp.max(s, axis=-1, keepdims=True)
        return row_max

    def value_stage(row_max, v2):
        for head in range(2):
            for c in range(n_chunks):
                lo = head * ATTN_BLOCK + c * ATTN_ROW_CHUNK
                p_ref[lo:lo + ATTN_ROW_CHUNK, :] = jnp.exp2(
                    s_ref[lo:lo + ATTN_ROW_CHUNK, :] - row_max[head][c]).astype(BF16)
        v_ones = jnp.concatenate([v2.astype(BF16), jnp.ones((2 * ATTN_BLOCK, LANES), BF16)], axis=1)
        pv = jnp.dot(p_ref[...], v_ones, preferred_element_type=F32)
        acc_b = jnp.where(head0, pv[:ATTN_BLOCK, :LANES], pv[ATTN_BLOCK:, :LANES])
        l_b = jnp.where(head0, pv[:ATTN_BLOCK, LANES:], pv[ATTN_BLOCK:, LANES:])
        m_b = jnp.concatenate([jnp.where(head0, row_max[0][c], row_max[1][c]) for c in range(n_chunks)],
                              axis=0)
        return m_b, l_b, acc_b

    def merged(rows, m_b, l_b, acc_b):
        m_old = m_ref[rows, :]
        m_new = jnp.maximum(m_old, m_b)
        w_old = jnp.exp2(m_old - m_new)
        w_blk = jnp.exp2(m_b - m_new)
        return (m_new, w_old * l_ref[rows, :] + w_blk * l_b, w_old * acc_ref[rows, :] + w_blk * acc_b)

    blocks = []

    for r16 in range(MAX_DILATION):
        r4, j = r16 % MID_DILATION, r16 // MID_DILATION

        def load_qk(r4=r4, j=j):
            return (q4_ref[r4, pl.ds(j, ATTN_BLOCK, stride=MID_DILATION), :],
                    k4_ref[r4, pl.ds(j, 2 * ATTN_BLOCK, stride=MID_DILATION), :])

        def load_v(r4=r4, j=j):
            return v4_ref[r4, pl.ds(j, 2 * ATTN_BLOCK, stride=MID_DILATION), :]

        def consume(m_b, l_b, acc_b, r16=r16):
            rows = pl.ds(r16, ATTN_BLOCK, stride=MAX_DILATION)
            m_ref[rows, :] = m_b
            l_ref[rows, :] = l_b
            acc_ref[rows, :] = acc_b

        blocks.append((load_qk, 1, load_v, consume))

    for r in range(MID_DILATION):
        for nb in range(n4 // ATTN_BLOCK):
            lo = nb * ATTN_BLOCK
            keys = slice(n4 - ATTN_BLOCK + lo, n4 + ATTN_BLOCK + lo)

            def load_qk(r=r, lo=lo, keys=keys):
                return q4_ref[r, lo:lo + ATTN_BLOCK, :], k4_ref[r, keys, :]

            def load_v(r=r, keys=keys):
                return v4_ref[r, keys, :]

            def consume(m_b, l_b, acc_b, r=r, lo=lo):
                rows = pl.ds(r + MID_DILATION * lo, ATTN_BLOCK, stride=MID_DILATION)
                m_new, l_new, acc_new = merged(rows, m_b, l_b, acc_b)
                m_ref[rows, :] = m_new
                l_ref[rows, :] = l_new
                acc_ref[rows, :] = acc_new

            blocks.append((load_qk, 1 if nb == 0 else 0, load_v, consume))

    for nb in range(ATTN_TILE // ATTN_BLOCK):
        lo = nb * ATTN_BLOCK

        def with_prev(cur_ref, prev_ref, lo=lo):
            if lo == 0:
                return jnp.concatenate([prev_ref[0, slab, ATTN_TILE - ATTN_BLOCK:, :],
                                        cur_ref[0, slab, :ATTN_BLOCK, :]], axis=0)
            return cur_ref[0, slab, lo - ATTN_BLOCK:lo + ATTN_BLOCK, :]

        def load_qk(lo=lo, with_prev=with_prev):
            return q_ref[0, slab, lo:lo + ATTN_BLOCK, :], with_prev(kc_ref, kp_ref)

        def load_v(with_prev=with_prev):
            return with_prev(vc_ref, vp_ref)

        def consume(m_b, l_b, acc_b, lo=lo):
            _, l_new, acc_new = merged(pl.ds(lo, ATTN_BLOCK), m_b, l_b, acc_b)
            o_ref[0, lo:lo + ATTN_BLOCK, slab * LANES:(slab + 1) * LANES] = (
                acc_new / l_new).astype(o_ref.dtype)

        blocks.append((load_qk, 1 if nb == 0 else 0, load_v, consume))

    def run_block(load_qk, which_bias, load_v, consume):
        row_max = score_stage(*load_qk(), which_bias)
        consume(*value_stage(row_max, load_v()))

    return [functools.partial(run_block, *block) for block in blocks]


def _attn_kernel(*refs):
    for slab in range(ATTN_SLABS_PER_STEP):
        for step in _attn_steps(pl.program_id(2) == 0, slab, *refs):
            step()


def _attn_call(q, k, v):
    B, _, S, _ = q.shape
    sps = ATTN_SLABS_PER_STEP
    cur = pl.BlockSpec((1, sps, ATTN_TILE, LANES), lambda b, s, i: (b, s, i, 0))
    prev = pl.BlockSpec((1, sps, ATTN_TILE, LANES), lambda b, s, i: (b, s, jnp.maximum(i - 1, 0), 0))
    n4 = ATTN_TILE // MID_DILATION
    return pl.pallas_call(
        _attn_kernel,
        grid=(B, N_SLABS // sps, S // ATTN_TILE),
        in_specs=[cur, cur, prev, cur, prev],
        out_specs=pl.BlockSpec((1, ATTN_TILE, sps * LANES), lambda b, s, i: (b, i, s)),
        out_shape=jax.ShapeDtypeStruct((B, S, ATTN_WIDTH), BF16),
        scratch_shapes=[pltpu.VMEM((2, ATTN_BLOCK, 2 * ATTN_BLOCK), F32),
                        pltpu.VMEM((MID_DILATION, n4, LANES), F32),
                        pltpu.VMEM((MID_DILATION, 2 * n4, LANES), F32),
                        pltpu.VMEM((MID_DILATION, 2 * n4, LANES), F32),
                        pltpu.VMEM((ATTN_TILE, LANES), F32),
                        pltpu.VMEM((ATTN_TILE, LANES), F32),
                        pltpu.VMEM((ATTN_TILE, LANES), F32),
                        pltpu.VMEM((2 * ATTN_BLOCK, 2 * ATTN_BLOCK), F32),
                        pltpu.VMEM((2 * ATTN_BLOCK, 2 * ATTN_BLOCK), BF16)],
        compiler_params=pltpu.CompilerParams(
            dimension_semantics=("parallel", "parallel", "parallel"), vmem_limit_bytes=VMEM_LIMIT),
        name="dilated_attn",
    )(q, k, k, v, v)


def _hgrn_scan_matrices():
    C = HGRN_CHUNK
    t = np.arange(C)[:, None]
    u = np.arange(C)[None, :]
    mats = [np.broadcast_to(u <= t, (C, C))]
    for lvl in range(HGRN_SMALL_LEVELS):
        start = (t >> lvl) << lvl
        end = start + (1 << lvl) - 1
        is_query = ((t >> lvl) & 1) == 1
        mats.append(np.where(is_query, (u >= start) & (u <= t), (u > t) & (u <= end)))
    return np.concatenate(mats, axis=0).astype(np.float32)


def _hgrn_steps(first_step, hq_ref, hf_ref, hi_ref, hg_ref, loglb_ref, log1mlb_ref, onemlb_ref, nw_ref,
                scan_ref, o_ref, state_ref):
    C = HGRN_CHUNK

    @pl.when(first_step)
    def _():
        state_ref[...] = jnp.zeros_like(state_ref)

    t_idx = lax.broadcasted_iota(jnp.int32, (C, C), 0)
    s_idx = lax.broadcasted_iota(jnp.int32, (C, C), 1)
    differ = t_idx ^ s_idx
    level = jnp.full((C, C), -1, jnp.int32)
    for lvl in range(HGRN_LEVELS):
        level = level + ((differ >> lvl) > 0).astype(jnp.int32)
    level = jnp.where(t_idx < s_idx, -2, level)

    def nt_dot(a, b):
        return lax.dot_general(a, b, NT_DIMS, preferred_element_type=F32)

    items = [(c, h) for c in range(hq_ref.shape[2] // C) for h in range(hq_ref.shape[1])]
    gates, partial, start_states = {}, {}, {}

    def gate_step(c, head):
        rows = slice(c * C, (c + 1) * C)
        log_lb, log1m_lb, one_m_lb = loglb_ref[head], log1mlb_ref[head], onemlb_ref[head]
        fp = hf_ref[0, head, rows, :]
        hq = hq_ref[0, head, rows, :].astype(F32)

        t = fp * LOG2_E
        log_sig = jnp.minimum(t, 0.0) - jnp.log2(1.0 + jnp.exp2(-jnp.abs(t)))
        l2 = log1m_lb + log_sig
        a = jnp.maximum(log_lb, l2) + jnp.log2(1.0 + jnp.exp2(-jnp.abs(log_lb - l2)))
        k = one_m_lb * jnp.exp2(log_sig - t)
        q = hq / (1.0 + jnp.exp2(hq * (-LOG2_E)))

        a_hi = a.astype(BF16)
        rest = a - a_hi.astype(F32)
        a_mid = rest.astype(BF16)
        a_lo = (rest - a_mid.astype(F32)).astype(BF16)
        b3 = jnp.dot(scan_ref[:C, :], jnp.concatenate([a_hi, a_mid, a_lo], axis=1),
                     preferred_element_type=F32)
        b = b3[:, :LANES] + b3[:, LANES:2 * LANES] + b3[:, 2 * LANES:]
        small = jnp.dot(scan_ref[C:, :], jnp.concatenate([a_hi, a_mid], axis=1),
                        preferred_element_type=F32)
        small = small[:, :LANES] + small[:, LANES:]
        gates[c, head] = (q.astype(BF16), k.astype(BF16), b, small)

    def segment_sums(b, small, lvl):
        if lvl < HGRN_SMALL_LEVELS:
            return small[lvl * C:(lvl + 1) * C]
        w = 1 << lvl
        parts = []
        for pair in range(C // (2 * w)):
            lo = pair * 2 * w
            boundary = b[lo + w - 1:lo + w, :]
            parts += [boundary - b[lo:lo + w, :], b[lo + w:lo + 2 * w, :] - boundary]
        return jnp.concatenate(parts, axis=0)

    def score_step(c, head):
        q_bf, k_bf, b, small = gates[c, head]
        v = hi_ref[0, head, c * C:(c + 1) * C, :]
        attn = jnp.where(level == -1, nt_dot(q_bf, k_bf), 0.0)
        for lvl in range(HGRN_LEVELS):
            factor = jnp.exp2(segment_sums(b, small, lvl)).astype(BF16)
            attn = jnp.where(level == lvl, nt_dot(q_bf * factor, k_bf * factor), attn)
        o_intra = jnp.dot(attn.astype(BF16), v, preferred_element_type=F32)
        q_in = q_bf * jnp.exp2(b).astype(BF16)
        k_out = k_bf * jnp.exp2(b[C - 1:C, :] - b).astype(BF16)
        update = lax.dot_general(v, k_out, TN_DIMS, preferred_element_type=F32)
        partial[c, head] = (o_intra, q_in, update, jnp.exp2(b[C - 1:C, :]))

    def state_step():
        for c, head in items:
            _, _, update, chunk_decay = partial[c, head]
            state = state_ref[head]
            start_states[c, head] = state.astype(BF16)
            state_ref[head] = state * chunk_decay + update

    def output_step(c, head):
        o_intra, q_in, _, _ = partial[c, head]
        rows = slice(c * C, (c + 1) * C)
        hg = hg_ref[0, head, rows, :].astype(F32)
        o = o_intra + nt_dot(q_in, start_states[c, head])
        o = o * lax.rsqrt(jnp.mean(o * o, axis=-1, keepdims=True) + RMS_EPS)
        o = o * nw_ref[head] * (hg / (1.0 + jnp.exp2(hg * (-LOG2_E))))
        o_ref[0, rows, head * LANES:(head + 1) * LANES] = o.astype(o_ref.dtype)

    return ([functools.partial(gate_step, *item) for item in items]
            + [functools.partial(score_step, *item) for item in items]
            + [state_step]
            + [functools.partial(output_step, *item) for item in items])


def _hgrn_kernel(*refs):
    for step in _hgrn_steps(pl.program_id(1) == 0, *refs):
        step()


def _hgrn_call(hq, hf, hi, hg, log_lb, log1m_lb, one_m_lb, norm_w):
    B, H, S, _ = hq.shape
    tile = pl.BlockSpec((1, H, HGRN_TILE, LANES), lambda b, i: (b, 0, i, 0))
    per_head = _resident((H, 1, LANES))
    scan = jnp.asarray(_hgrn_scan_matrices(), dtype=BF16)
    return pl.pallas_call(
        _hgrn_kernel,
        grid=(B, S // HGRN_TILE),
        in_specs=[tile, tile, tile, tile, per_head, per_head, per_head, per_head,
                  _resident(scan.shape)],
        out_specs=pl.BlockSpec((1, HGRN_TILE, H * LANES), lambda b, i: (b, i, 0)),
        out_shape=jax.ShapeDtypeStruct((B, S, HGRN_WIDTH), BF16),
        scratch_shapes=[pltpu.VMEM((H, LANES, LANES), F32)],
        compiler_params=pltpu.CompilerParams(
            dimension_semantics=("parallel", "arbitrary"), vmem_limit_bytes=VMEM_LIMIT),
        name="hgrn2",
    )(hq, hf, hi, hg, log_lb, log1m_lb, one_m_lb, norm_w, scan)


def kernel(x, c, positions, ln_g, ln_b, ada_w, ada_b, ffn1_w_in, ffn1_w_out, ffn2_w_in, ffn2_w_out,
           mix_w_in, mix_w_out, hgrn_norm_w, hgrn_lb_logits):
    B, S, D = x.shape
    assert D == D_MODEL and S % ATTN_TILE == 0 and S % HGRN_TILE == 0
    assert S % (FFN_GROUPS * FFN_GROUP_ROWS) == 0 and S % ROW_TILE == 0

    lb_all = jnp.cumsum(jax.nn.softmax(hgrn_lb_logits.astype(F32), axis=0), axis=0)
    lb_all = (lb_all - lb_all[0:1]).reshape(DEPTH, N_SLABS, 1, LANES)
    log_lb, log1m_lb, one_m_lb = jnp.log(lb_all) * LOG2_E, jnp.log1p(-lb_all) * LOG2_E, 1.0 - lb_all
    norm_w = hgrn_norm_w.astype(F32).reshape(DEPTH, N_SLABS, 1, LANES)

    ada = _ada_call(c, ada_w, ada_b).reshape(DEPTH, B, N_SUBLAYERS, 3, 1, D)
    cos, sin_up, sin_dn = _rope_call(positions)

    ffn1_in, ffn1_out, ffn2_in, ffn2_out, mix_in, mix_out = (
        w.astype(BF16) for w in (ffn1_w_in, ffn1_w_out, ffn2_w_in, ffn2_w_out, mix_w_in, mix_w_out))

    for l in range(DEPTH):
        def mod(sub, kind, l=l):
            return ada[l, :, sub, kind]

        x = _ffn_call(x, mod(0, 0), mod(0, 1), mod(0, 2), ffn1_in, ffn1_out, ln_g[l, 0], ln_b[l, 0], l)

        q, k, v, hq, hf, hi, hg = _proj_call(x, mod(1, 0), mod(1, 1), mix_in, cos, sin_up, sin_dn, l)
        ao = _attn_call(q, k, v)
        go = _hgrn_call(hq, hf, hi, hg, log_lb[l], log1m_lb[l], one_m_lb[l], norm_w[l])
        x = _mix_out_ffn_call(x, ao, go, mod(1, 2), mix_out, ln_g[l, 1], ln_b[l, 1],
                              mod(2, 0), mod(2, 1), mod(2, 2), ffn2_in, ffn2_out,
                              ln_g[l, 2], ln_b[l, 2], l)
    return x
```
